```python
import jax
import jax.numpy as jnp
from jax import lax
import numpy as np

D_MODEL = 1024
BATCH = 2
SEQ = 16384
DEPTH = 2

N_MIXERS = 2
N_META = 16
BLOCK = 128
EPS = 1e-6
NEG_INF = -1e30

FOX_HEADS = 16
FOX_HEAD_DIM = D_MODEL // FOX_HEADS
FOX_IN = 3 * D_MODEL + FOX_HEADS
FOX_FORGET_BIAS = 2.0

HG_HEADS = 8
HG_KEY_DIM = 128
HG_VAL_DIM = D_MODEL // HG_HEADS
HG_KEY_TOTAL = HG_HEADS * HG_KEY_DIM
HG_IN = 2 * HG_KEY_TOTAL + 2 * D_MODEL

N_GROUPS = 4
EXPERTS_PER_GROUP = 8
N_EXPERTS = N_GROUPS * EXPERTS_PER_GROUP
TOP_K = 2
D_EXPERT = D_MODEL // 2
EXPERT_ROW_BLOCK = 128

kernel_name = 'hybrid_fox_hgrn2_hmoe'


def rms_norm(x, gain):
    xf = x.astype(jnp.float32)
    y = xf * lax.rsqrt(jnp.mean(xf * xf, axis=-1, keepdims=True) + EPS)
    return (y * gain.astype(jnp.float32)).astype(x.dtype)


def front_pad(t, pad):
    widths = [(0, 0)] * t.ndim
    widths[1] = (pad, 0)
    return jnp.pad(t, widths)


def fox_attention(h, w_in, b_f, g_q, g_k, w_out):
    bsz, length, _ = h.shape
    q, k, v, f_logit = jnp.split(h @ w_in, [D_MODEL, 2 * D_MODEL, 3 * D_MODEL], axis=-1)
    q = rms_norm(q.reshape(bsz, length, FOX_HEADS, FOX_HEAD_DIM), g_q)
    k = rms_norm(k.reshape(bsz, length, FOX_HEADS, FOX_HEAD_DIM), g_k)
    v = v.reshape(bsz, length, FOX_HEADS, FOX_HEAD_DIM)
    log_f = jax.nn.log_sigmoid(f_logit.astype(jnp.float32) + b_f.astype(jnp.float32))
    pad = (-length) % BLOCK
    q, k, v, log_f = (front_pad(t, pad) for t in (q, k, v, log_f))
    lp = length + pad
    n_blk = lp // BLOCK
    cum = jnp.cumsum(log_f, axis=1).transpose(0, 2, 1)
    kh = k.transpose(0, 2, 1, 3)
    vh = v.transpose(0, 2, 1, 3)
    qb = q.reshape(bsz, n_blk, BLOCK, FOX_HEADS, FOX_HEAD_DIM).transpose(1, 0, 3, 2, 4)
    cb = cum.reshape(bsz, FOX_HEADS, n_blk, BLOCK).transpose(2, 0, 1, 3)
    key_pos = jnp.arange(lp)
    key_ok = key_pos >= pad
    scale = FOX_HEAD_DIM ** -0.5

    def attend(args):
        qi, ci, bi = args
        s = jnp.einsum('bhqd,bhkd->bhqk', qi, kh).astype(jnp.float32) * scale
        s = s + ci[..., :, None] - cum[:, :, None, :]
        q_pos = bi * BLOCK + jnp.arange(BLOCK)
        mask = (key_pos[None, :] <= q_pos[:, None]) & key_ok[None, :]
        p = jax.nn.softmax(jnp.where(mask, s, NEG_INF), axis=-1)
        return jnp.einsum('bhqk,bhkd->bhqd', p.astype(vh.dtype), vh)

    o = lax.map(attend, (qb, cb, jnp.arange(n_blk)))
    o = o.transpose(1, 0, 3, 2, 4).reshape(bsz, lp, D_MODEL)[:, pad:]
    return o @ w_out


def hgrn_lower_bounds(lb_logits):
    p = jax.nn.softmax(lb_logits.astype(jnp.float32), axis=0)
    cum = jnp.cumsum(p, axis=0)
    return cum - cum[:1]


def hgrn2_recurrence(h, w_in, b_f, lower_bound, g_o, w_out):
    bsz, length, _ = h.shape
    q, f_logit, v, g = jnp.split(
        h @ w_in, [HG_KEY_TOTAL, 2 * HG_KEY_TOTAL, 2 * HG_KEY_TOTAL + D_MODEL], axis=-1)
    f = lower_bound + (1.0 - lower_bound) * jax.nn.sigmoid(
        f_logit.astype(jnp.float32) + b_f.astype(jnp.float32))
    log_f = jnp.log(f)
    k = 1.0 - f
    q = jax.nn.silu(q.astype(jnp.float32))
    v = v.astype(jnp.float32)
    pad = (-length) % BLOCK
    lp = length + pad
    n_chk = lp // BLOCK

    def to_chunks(t, d):
        t = front_pad(t, pad).reshape(bsz, n_chk, BLOCK, HG_HEADS, d)
        return t.transpose(1, 0, 3, 2, 4)

    qc = to_chunks(q, HG_KEY_DIM)
    kc = to_chunks(k, HG_KEY_DIM)
    vc = to_chunks(v, HG_VAL_DIM)
    lc = to_chunks(log_f, HG_KEY_DIM)
    causal = jnp.tril(jnp.ones((BLOCK, BLOCK), dtype=bool))

    def step(state, args):
        qi, ki, vi, li = args
        b = jnp.cumsum(li, axis=2)
        o_inter = jnp.einsum('bhtk,bhkv->bhtv', qi * jnp.exp(b), state)
        diff = b[:, :, :, None, :] - b[:, :, None, :, :]
        decay = jnp.exp(jnp.where(causal[:, :, None], diff, -jnp.inf))
        scores = jnp.einsum('bhtk,bhsk,bhtsk->bhts', qi, ki, decay)
        o_intra = jnp.einsum('bhts,bhsv->bhtv', scores, vi)
        b_end = b[:, :, -1:, :]
        state = (jnp.exp(b_end[:, :, 0, :])[..., None] * state
                 + jnp.einsum('bhsk,bhsv->bhkv', ki * jnp.exp(b_end - b), vi))
        return state, o_inter + o_intra

    s0 = jnp.zeros((bsz, HG_HEADS, HG_KEY_DIM, HG_VAL_DIM), jnp.float32)
    _, o = lax.scan(step, s0, (qc, kc, vc, lc))
    o = o.transpose(1, 0, 3, 2, 4).reshape(bsz, lp, HG_HEADS, HG_VAL_DIM)[:, pad:]
    o = rms_norm(o, g_o).reshape(bsz, length, D_MODEL) * jax.nn.silu(g.astype(jnp.float32))
    return o.astype(h.dtype) @ w_out


def hier_moe(h, w_grp, b_grp, w_rt, b_rt, w_up, w_down):
    bsz, length, dm = h.shape
    n_tok = bsz * length
    n_asg = n_tok * TOP_K
    xt = h.reshape(n_tok, dm)
    g_prob = jax.nn.softmax((xt @ w_grp).astype(jnp.float32) + b_grp.astype(jnp.float32), axis=-1)
    g_gate, g_idx = lax.top_k(g_prob, 1)
    e_logit = ((xt @ w_rt).astype(jnp.float32) + b_rt.astype(jnp.float32)).reshape(
        n_tok, N_GROUPS, EXPERTS_PER_GROUP)
    e_logit = jnp.take_along_axis(e_logit, g_idx[:, :, None], axis=1)[:, 0]
    e_val, e_idx = lax.top_k(e_logit, TOP_K)
    gate = jax.nn.softmax(e_val, axis=-1) * g_gate
    expert = (g_idx * EXPERTS_PER_GROUP + e_idx).reshape(-1)
    order = jnp.argsort(expert)
    se = expert[order]
    tok = order // TOP_K
    w_sorted = gate.reshape(-1)[order]
    counts = jnp.bincount(expert, length=N_EXPERTS)
    padded = (counts + EXPERT_ROW_BLOCK - 1) // EXPERT_ROW_BLOCK * EXPERT_ROW_BLOCK
    start = jnp.cumsum(counts) - counts
    pend = jnp.cumsum(padded)
    pstart = pend - padded
    dest = pstart[se] + jnp.arange(n_asg) - start[se]
    n_blocks = (n_asg + N_EXPERTS * (EXPERT_ROW_BLOCK - 1) + EXPERT_ROW_BLOCK - 1) // EXPERT_ROW_BLOCK
    rows = n_blocks * EXPERT_ROW_BLOCK
    x_pad = jnp.zeros((rows, dm), h.dtype).at[dest].set(xt[tok])
    block_expert = jnp.minimum(
        jnp.searchsorted(pend, jnp.arange(n_blocks) * EXPERT_ROW_BLOCK, side='right'), N_EXPERTS - 1)

    def expert_block(args):
        xb, e = args
        gate_up = xb @ w_up[e]
        a, u = jnp.split(gate_up, 2, axis=-1)
        return (jax.nn.silu(a) * u) @ w_down[e]

    y_pad = lax.map(expert_block, (x_pad.reshape(n_blocks, EXPERT_ROW_BLOCK, dm), block_expert))
    y = y_pad.reshape(rows, dm)[dest] * w_sorted[:, None].astype(h.dtype)
    return jnp.zeros_like(xt).at[tok].add(y).reshape(bsz, length, dm)


def setup_inputs(seed: int = 0) -> dict:
    key = jax.random.key(seed)
    ks = jax.random.split(key, 24)
    n_fox = (DEPTH + N_MIXERS - 1) // N_MIXERS
    n_hg = DEPTH // N_MIXERS
    out_scale = (2 * DEPTH) ** -0.5

    def nrm(k, shape, scale):
        return jax.random.normal(k, shape, jnp.float32) * scale

    def gain(k, shape):
        return 1.0 + nrm(k, shape, 0.02)

    return {
        'x': nrm(ks[0], (BATCH, SEQ, D_MODEL), 1.0),
        'meta_tokens': nrm(ks[1], (N_META, D_MODEL), 1.0),
        'fox_norm': gain(ks[2], (n_fox, D_MODEL)),
        'fox_w_in': nrm(ks[3], (n_fox, D_MODEL, FOX_IN), D_MODEL ** -0.5),
        'fox_b_f': FOX_FORGET_BIAS + nrm(ks[4], (n_fox, FOX_HEADS), 0.1),
        'fox_q_norm': gain(ks[5], (n_fox, FOX_HEAD_DIM)),
        'fox_k_norm': gain(ks[6], (n_fox, FOX_HEAD_DIM)),
        'fox_w_out': nrm(ks[7], (n_fox, D_MODEL, D_MODEL), D_MODEL ** -0.5 * out_scale),
        'hg_norm': gain(ks[8], (n_hg, D_MODEL)),
        'hg_w_in': nrm(ks[9], (n_hg, D_MODEL, HG_IN), D_MODEL ** -0.5),
        'hg_b_f': nrm(ks[10], (n_hg, HG_KEY_TOTAL), 0.02),
        'hg_lb_logits': nrm(ks[11], (DEPTH, HG_KEY_TOTAL), 0.5),
        'hg_o_norm': gain(ks[12], (n_hg, HG_VAL_DIM)),
        'hg_w_out': nrm(ks[13], (n_hg, D_MODEL, D_MODEL), D_MODEL ** -0.5 * out_scale),
        'moe_norm': gain(ks[14], (DEPTH, D_MODEL)),
        'moe_w_grp': nrm(ks[15], (DEPTH, D_MODEL, N_GROUPS), D_MODEL ** -0.5),
        'moe_b_grp': nrm(ks[16], (DEPTH, N_GROUPS), 0.01),
        'moe_w_rt': nrm(ks[17], (DEPTH, D_MODEL, N_EXPERTS), D_MODEL ** -0.5),
        'moe_b_rt': nrm(ks[18], (DEPTH, N_EXPERTS), 0.01),
        'moe_w_up': nrm(ks[19], (DEPTH, N_EXPERTS, D_MODEL, 2 * D_EXPERT), D_MODEL ** -0.5),
        'moe_w_down': nrm(ks[20], (DEPTH, N_EXPERTS, D_EXPERT, D_MODEL), D_EXPERT ** -0.5 * out_scale),
    }


def reference(x, meta_tokens, fox_norm, fox_w_in, fox_b_f, fox_q_norm, fox_k_norm, fox_w_out,
              hg_norm, hg_w_in, hg_b_f, hg_lb_logits, hg_o_norm, hg_w_out,
              moe_norm, moe_w_grp, moe_b_grp, moe_w_rt, moe_b_rt, moe_w_up, moe_w_down):
    bsz = x.shape[0]
    meta = jnp.broadcast_to(meta_tokens[None].astype(x.dtype), (bsz, N_META, D_MODEL))
    h = jnp.concatenate([meta, x], axis=1)
    lower_bounds = hgrn_lower_bounds(hg_lb_logits)
    for i in range(DEPTH):
        j = i // N_MIXERS
        if i % N_MIXERS == 0:
            h = h + fox_attention(rms_norm(h, fox_norm[j]), fox_w_in[j], fox_b_f[j],
                                  fox_q_norm[j], fox_k_norm[j], fox_w_out[j])
        else:
            h = h + hgrn2_recurrence(rms_norm(h, hg_norm[j]), hg_w_in[j], hg_b_f[j],
                                     lower_bounds[i], hg_o_norm[j], hg_w_out[j])
        h = h + hier_moe(rms_norm(h, moe_norm[i]), moe_w_grp[i], moe_b_grp[i],
                         moe_w_rt[i], moe_b_rt[i], moe_w_up[i], moe_w_down[i])
    return h[:, N_META:]
```

```python
import functools

import jax
import jax.numpy as jnp
from jax import lax
from jax.experimental import pallas as pl
from jax.experimental.pallas import tpu as pltpu

F32 = jnp.float32
BF16 = jnp.bfloat16
HIGHEST = lax.Precision.HIGHEST

D_MODEL = 1024
N_META = 16
ROW_ALIGN = 128
EPS = 1e-6
NEG_BIG = -1e30

FOX_HEADS = 16
FOX_HEAD_DIM = D_MODEL // FOX_HEADS
HG_HEADS = 8
HG_KEY_DIM = 128
HG_VAL_DIM = D_MODEL // HG_HEADS
HG_KEY_TOTAL = HG_HEADS * HG_KEY_DIM
HG_CHUNK = 128
HG_SUB = 16

N_GROUPS = 4
EXPERTS_PER_GROUP = 8
N_EXPERTS = N_GROUPS * EXPERTS_PER_GROUP
TOP_K = 2
D_EXPERT = D_MODEL // 2
EXPERT_ROWS = 256

LANES = 128
SUBLANES = 8
ROW_TILES = D_MODEL // LANES
VMEM_LIMIT = 56 * 1024 * 1024


def _dot(a, b, precision=None):
    return jnp.dot(a, b, preferred_element_type=F32, precision=precision)


def _dot_nt(a, b):
    return lax.dot_general(a, b, (((1,), (1,)), ((), ())), preferred_element_type=F32)


def _dot_tn(a, b):
    return lax.dot_general(a, b, (((0,), (0,)), ((), ())), preferred_element_type=F32)


def _rms(x, gain):
    return x * lax.rsqrt(jnp.mean(x * x, axis=-1, keepdims=True) + EPS) * gain


def _sigmoid(x):
    return 1.0 / (1.0 + jnp.exp(-x))


def _silu(x):
    return x * _sigmoid(x)


def _row_tile(lp):
    for t in (384, 256, 128):
        if lp % t == 0:
            return t
    raise ValueError(lp)


def _params(n_axes):
    return pltpu.CompilerParams(dimension_semantics=("arbitrary",) * n_axes,
                                vmem_limit_bytes=VMEM_LIMIT)


def _full(shape):
    return pl.BlockSpec(shape, lambda *_: (0,) * len(shape))


def _fox_in_kernel(h_ref, gain_ref, wqkv_ref, wf_ref, bf_ref, gq_ref, gk_ref, seg_ref, segt_ref,
                   tri_ref, q_ref, k_ref, v_ref, c_ref, carry_ref, *, tm, pad):
    t = pl.program_id(1)

    @pl.when(t == 0)
    def _():
        carry_ref[...] = jnp.zeros_like(carry_ref)

    xn = _rms(h_ref[0], gain_ref[...])
    qkv = _dot(xn.astype(BF16), wqkv_ref[...])

    def head_norm(z, g_ref):
        ms = _dot(z * z, seg_ref[...], HIGHEST) * (1.0 / FOX_HEAD_DIM)
        r = _dot(lax.rsqrt(ms + EPS), segt_ref[...], HIGHEST)
        return z * r * g_ref[...]

    q_ref[0] = head_norm(qkv[:, :D_MODEL], gq_ref).astype(BF16)
    k_ref[0] = head_norm(qkv[:, D_MODEL:2 * D_MODEL], gk_ref).astype(BF16)
    v_ref[0] = qkv[:, 2 * D_MODEL:].astype(BF16)

    fl = _dot(xn, wf_ref[...], HIGHEST) + bf_ref[...]
    log_f = jnp.minimum(fl, 0.0) - jnp.log(1.0 + jnp.exp(-jnp.abs(fl)))
    pos = t * tm + lax.broadcasted_iota(jnp.int32, log_f.shape, 0)
    log_f = jnp.where(pos >= pad, log_f, 0.0)
    cum = _dot(tri_ref[...], log_f, HIGHEST) + carry_ref[...]
    c_ref[0] = cum
    carry_ref[...] = cum[tm - 1:tm, :]


def _fox_in(h, gain, wqkv, wf, bf, gq, gk, pad):
    bsz, lp, _ = h.shape
    tm = _row_tile(lp)
    head_of = jnp.arange(D_MODEL) // FOX_HEAD_DIM
    seg = (head_of[:, None] == jnp.arange(FOX_HEADS)[None, :]).astype(F32)
    tri = jnp.tril(jnp.ones((tm, tm), F32))
    row = pl.BlockSpec((1, tm, D_MODEL), lambda b, t: (b, t, 0))
    return pl.pallas_call(
        functools.partial(_fox_in_kernel, tm=tm, pad=pad),
        grid=(bsz, lp // tm),
        in_specs=[row, _full((1, D_MODEL)), _full((D_MODEL, 3 * D_MODEL)),
                  _full((D_MODEL, FOX_HEADS)), _full((1, FOX_HEADS)),
                  _full((1, D_MODEL)), _full((1, D_MODEL)),
                  _full((D_MODEL, FOX_HEADS)), _full((FOX_HEADS, D_MODEL)), _full((tm, tm))],
        out_specs=[row, row, row, pl.BlockSpec((1, tm, FOX_HEADS), lambda b, t: (b, t, 0))],
        out_shape=[jax.ShapeDtypeStruct((bsz, lp, D_MODEL), BF16)] * 3
        + [jax.ShapeDtypeStruct((bsz, lp, FOX_HEADS), F32)],
        scratch_shapes=[pltpu.VMEM((1, FOX_HEADS), F32)],
        compiler_params=_params(2),
    )(h, gain, wqkv, wf, bf, gq, gk, seg, seg.T, tri)


def _flash_kernel(q_ref, k_ref, v_ref, o_ref, *, tq):
    i = pl.program_id(2)
    q = q_ref[0, 0]

    def block(j, m, acc, masked):
        kj = k_ref[0, 0, pl.ds(pl.multiple_of(j * tq, tq), tq), :]
        vj = v_ref[0, 0, pl.ds(pl.multiple_of(j * tq, tq), tq), :]
        s = _dot_nt(q, kj)
        if masked:
            r = lax.broadcasted_iota(jnp.int32, s.shape, 0)
            c = lax.broadcasted_iota(jnp.int32, s.shape, 1)
            s = jnp.where(c <= r, s, NEG_BIG)
        m_new = jnp.maximum(m, jnp.max(s, axis=-1, keepdims=True))
        p = jnp.exp(s - m_new)
        acc = acc * jnp.exp(m - m_new) + _dot(p.astype(BF16), vj)
        return m_new, acc

    m0 = jnp.full((tq, 1), NEG_BIG, F32)
    acc0 = jnp.zeros((tq, LANES), F32)
    m, acc = lax.fori_loop(0, i, lambda j, c: block(j, c[0], c[1], False), (m0, acc0))
    m, acc = block(i, m, acc, True)
    denom = acc[:, FOX_HEAD_DIM:FOX_HEAD_DIM + 1]
    o_ref[0, 0] = (acc / denom).astype(o_ref.dtype)


def _flash(qa, ka, va):
    bsz, nh, lp, _ = qa.shape
    tq = _row_tile(lp)
    qspec = pl.BlockSpec((1, 1, tq, LANES), lambda b, h, i: (b, h, i, 0))
    kvspec = pl.BlockSpec((1, 1, lp, LANES), lambda b, h, i: (b, h, 0, 0))
    return pl.pallas_call(
        functools.partial(_flash_kernel, tq=tq),
        grid=(bsz, nh, lp // tq),
        in_specs=[qspec, kvspec, kvspec],
        out_specs=qspec,
        out_shape=jax.ShapeDtypeStruct((bsz, nh, lp, LANES), BF16),
        compiler_params=_params(3),
    )(qa, ka, va)


def _bf16_trunc(x):
    bits = lax.bitcast_convert_type(x, jnp.uint32) & jnp.uint32(0xFFFF0000)
    return lax.bitcast_convert_type(bits, F32)


def _split3(c):
    hi = _bf16_trunc(c)
    mid = _bf16_trunc(c - hi)
    lo = c - hi - mid
    return hi.astype(BF16), mid.astype(BF16), lo.astype(BF16)


def _fox_augment(q, k, v, cum, pad):
    bsz, lp, _ = q.shape

    def heads(t):
        return t.reshape(bsz, lp, FOX_HEADS, FOX_HEAD_DIM).transpose(0, 2, 1, 3)

    hi, mid, lo = (t.transpose(0, 2, 1)[..., None] for t in _split3(cum))
    one = jnp.ones_like(hi)
    key_pad = jnp.where(jnp.arange(lp) < pad, NEG_BIG, 0.0).astype(BF16)
    key_pad = jnp.broadcast_to(key_pad[None, None, :, None], hi.shape)
    fill_q = jnp.zeros((bsz, FOX_HEADS, lp, LANES - FOX_HEAD_DIM - 7), BF16)
    qa = jnp.concatenate([heads(q), hi, mid, lo, one, one, one, one, fill_q], axis=-1)
    ka = jnp.concatenate([heads(k), one, one, one, -hi, -mid, -lo, key_pad, fill_q], axis=-1)
    fill_v = jnp.zeros((bsz, FOX_HEADS, lp, LANES - FOX_HEAD_DIM - 1), BF16)
    va = jnp.concatenate([heads(v), one, fill_v], axis=-1)
    return qa, ka, va


def _out_proj_kernel(a_ref, w_ref, h_ref, o_ref, *, tm, pad):
    t = pl.program_id(1)
    y = h_ref[0] + _dot(a_ref[0], w_ref[...])
    pos = t * tm + lax.broadcasted_iota(jnp.int32, y.shape, 0)
    o_ref[0] = jnp.where(pos >= pad, y, 0.0)


def _out_proj(a, w, h, pad):
    bsz, lp, _ = h.shape
    tm = _row_tile(lp)
    row = pl.BlockSpec((1, tm, D_MODEL), lambda b, t: (b, t, 0))
    return pl.pallas_call(
        functools.partial(_out_proj_kernel, tm=tm, pad=pad),
        grid=(bsz, lp // tm),
        in_specs=[row, _full((D_MODEL, D_MODEL)), row],
        out_specs=row,
        out_shape=jax.ShapeDtypeStruct(h.shape, F32),
        compiler_params=_params(2),
    )(a, w, h)


def _hgrn_kernel(h_ref, gain_ref, win_ref, bf_ref, lb_ref, go_ref, wout_ref, tri_ref, o_ref,
                 state_ref, *, pad):
    t = pl.program_id(1)

    @pl.when(t == 0)
    def _():
        state_ref[...] = jnp.zeros_like(state_ref)

    x = h_ref[0]
    proj = _dot(_rms(x, gain_ref[...]).astype(BF16), win_ref[...])
    tri = tri_ref[...]
    row = lax.broadcasted_iota(jnp.int32, (HG_CHUNK, HG_KEY_DIM), 0)
    causal = (lax.broadcasted_iota(jnp.int32, (HG_CHUNK, HG_CHUNK), 1)
              <= lax.broadcasted_iota(jnp.int32, (HG_CHUNK, HG_CHUNK), 0))
    outs = []
    for hd in range(HG_HEADS):
        ksl = slice(hd * HG_KEY_DIM, (hd + 1) * HG_KEY_DIM)
        vsl = slice(hd * HG_VAL_DIM, (hd + 1) * HG_VAL_DIM)
        q = _silu(proj[:, ksl])
        lb = lb_ref[:, ksl]
        f = lb + (1.0 - lb) * _sigmoid(proj[:, HG_KEY_TOTAL:2 * HG_KEY_TOTAL][:, ksl] + bf_ref[:, ksl])
        log_f = jnp.log(f)
        kk = 1.0 - f
        v = proj[:, 2 * HG_KEY_TOTAL:2 * HG_KEY_TOTAL + D_MODEL][:, vsl]
        lf_hi = log_f.astype(BF16)
        lf_lo = (log_f - lf_hi.astype(F32)).astype(BF16)
        b = _dot(tri, lf_hi) + _dot(tri, lf_lo)
        state = state_ref[hd]
        o = _dot_nt(q * jnp.exp(b), state)
        blocks = []
        for sc in range(HG_CHUNK // HG_SUB):
            lo_r, hi_r = sc * HG_SUB, (sc + 1) * HG_SUB
            ref = b[lo_r - 1:lo_r, :] if sc else jnp.zeros((1, HG_KEY_DIM), F32)
            qt = q[lo_r:hi_r] * jnp.exp(b[lo_r:hi_r] - ref)
            kt = kk * jnp.exp(jnp.where(row < hi_r, ref - b, NEG_BIG))
            blocks.append(_dot_nt(qt, kt))
        scores = jnp.where(causal, jnp.concatenate(blocks, axis=0), 0.0)
        o = o + _dot(scores, v)
        b_end = b[HG_CHUNK - 1:HG_CHUNK, :]
        state_ref[hd] = jnp.exp(b_end) * state + _dot_tn(v, kk * jnp.exp(b_end - b))
        outs.append(_rms(o, go_ref[...]))
    gate = _silu(proj[:, 2 * HG_KEY_TOTAL + D_MODEL:])
    y = x + _dot((jnp.concatenate(outs, axis=-1) * gate).astype(BF16), wout_ref[...])
    pos = t * HG_CHUNK + lax.broadcasted_iota(jnp.int32, y.shape, 0)
    o_ref[0] = jnp.where(pos >= pad, y, 0.0)


def _hgrn_layer(h, gain, win, bf, lb, go, wout, pad):
    bsz, lp, _ = h.shape
    hg_in = win.shape[1]
    tri = jnp.tril(jnp.ones((HG_CHUNK, HG_CHUNK), BF16))
    row = pl.BlockSpec((1, HG_CHUNK, D_MODEL), lambda b, t: (b, t, 0))
    return pl.pallas_call(
        functools.partial(_hgrn_kernel, pad=pad),
        grid=(bsz, lp // HG_CHUNK),
        in_specs=[row, _full((1, D_MODEL)), _full((D_MODEL, hg_in)), _full((1, HG_KEY_TOTAL)),
                  _full((1, HG_KEY_TOTAL)), _full((1, HG_VAL_DIM)), _full((D_MODEL, D_MODEL)),
                  _full((HG_CHUNK, HG_CHUNK))],
        out_specs=row,
        out_shape=jax.ShapeDtypeStruct(h.shape, F32),
        scratch_shapes=[pltpu.VMEM((HG_HEADS, HG_VAL_DIM, HG_KEY_DIM), F32)],
        compiler_params=_params(2),
    )(h, gain, win, bf, lb, go, wout, tri)


def _router_kernel(h_ref, gain_ref, w_ref, b_ref, idx_ref, gate_ref, xn_ref):
    xn = _rms(h_ref[...], gain_ref[...])
    for s in range(ROW_TILES):
        xn_ref[:, s, :] = xn[:, s * LANES:(s + 1) * LANES]
    logit = _dot(xn, w_ref[...], HIGHEST) + b_ref[...]
    lane = lax.broadcasted_iota(jnp.int32, logit.shape, 1)
    is_grp = (lane >= N_EXPERTS) & (lane < N_EXPERTS + N_GROUPS)
    g_logit = jnp.where(is_grp, logit, -jnp.inf)
    g_max = jnp.max(g_logit, axis=-1, keepdims=True)
    g_idx = jnp.min(jnp.where(g_logit == g_max, lane, 4 * LANES), axis=-1, keepdims=True) - N_EXPERTS
    g_gate = 1.0 / jnp.sum(jnp.exp(g_logit - g_max), axis=-1, keepdims=True)
    in_grp = (lane >= g_idx * EXPERTS_PER_GROUP) & (lane < (g_idx + 1) * EXPERTS_PER_GROUP)
    e1 = jnp.where(in_grp, logit, -jnp.inf)
    v1 = jnp.max(e1, axis=-1, keepdims=True)
    i1 = jnp.min(jnp.where(e1 == v1, lane, 4 * LANES), axis=-1, keepdims=True)
    e2 = jnp.where(lane == i1, -jnp.inf, e1)
    v2 = jnp.max(e2, axis=-1, keepdims=True)
    i2 = jnp.min(jnp.where(e2 == v2, lane, 4 * LANES), axis=-1, keepdims=True)
    z = jnp.exp(v2 - v1)
    w1 = g_gate / (1.0 + z)
    idx_ref[...] = jnp.where(lane == 0, i1, jnp.where(lane == 1, i2, 0))
    gate_ref[...] = jnp.where(lane == 0, w1, jnp.where(lane == 1, w1 * z, 0.0))


def _router(h2, gain, w_cat, b_cat):
    n = h2.shape[0]
    tm = _row_tile(n)
    row = pl.BlockSpec((tm, D_MODEL), lambda t: (t, 0))
    lanes = pl.BlockSpec((tm, LANES), lambda t: (t, 0))
    return pl.pallas_call(
        _router_kernel,
        grid=(n // tm,),
        in_specs=[row, _full((1, D_MODEL)), _full((D_MODEL, LANES)), _full((1, LANES))],
        out_specs=[lanes, lanes, pl.BlockSpec((tm, ROW_TILES, LANES), lambda t: (t, 0, 0))],
        out_shape=[jax.ShapeDtypeStruct((n, LANES), jnp.int32),
                   jax.ShapeDtypeStruct((n, LANES), F32),
                   jax.ShapeDtypeStruct((n, ROW_TILES, LANES), F32)],
        compiler_params=_params(1),
    )(h2, gain, w_cat, b_cat)


def _gather_copy(src_hbm, dst_ref, sem, src_row, dst_row):
    return pltpu.make_async_copy(src_hbm.at[pl.ds(src_row, 1)], dst_ref.at[pl.ds(dst_row, 1)], sem)


def _gather_kernel(idx_ref, src_hbm, o_ref, sem, *, rows):
    def issue(r, carry):
        _gather_copy(src_hbm, o_ref, sem, idx_ref[0, 0, r], r).start()
        return carry

    lax.fori_loop(0, rows, issue, 0)
    pltpu.make_async_copy(src_hbm.at[pl.ds(0, rows)], o_ref, sem).wait()


def _gather_rows(src3, idx, rows_per_step):
    n_out = idx.shape[0]
    steps = n_out // rows_per_step
    return pl.pallas_call(
        functools.partial(_gather_kernel, rows=rows_per_step),
        grid=(steps,),
        in_specs=[pl.BlockSpec((1, 1, rows_per_step), lambda i: (i, 0, 0), memory_space=pltpu.SMEM),
                  pl.BlockSpec(memory_space=pl.ANY)],
        out_specs=pl.BlockSpec((rows_per_step, ROW_TILES, LANES), lambda i: (i, 0, 0)),
        out_shape=jax.ShapeDtypeStruct((n_out, ROW_TILES, LANES), F32),
        scratch_shapes=[pltpu.SemaphoreType.DMA(())],
        compiler_params=_params(1),
    )(idx.reshape(steps, 1, rows_per_step), src3)


def _expert_kernel(be_ref, x_ref, wup_ref, wdn_ref, y_ref):
    del be_ref
    x = jnp.concatenate([x_ref[:, s, :] for s in range(ROW_TILES)], axis=-1)
    gu = _dot(x.astype(BF16), wup_ref[0])
    act = _silu(gu[:, :D_EXPERT]) * gu[:, D_EXPERT:]
    y = _dot(act.astype(BF16), wdn_ref[0])
    for s in range(ROW_TILES):
        y_ref[:, s, :] = y[:, s * LANES:(s + 1) * LANES]


def _experts(x_pad3, block_expert, w_up, w_down):
    rows = x_pad3.shape[0]
    blk = pl.BlockSpec((EXPERT_ROWS, ROW_TILES, LANES), lambda i, be: (i, 0, 0))
    return pl.pallas_call(
        _expert_kernel,
        grid_spec=pltpu.PrefetchScalarGridSpec(
            num_scalar_prefetch=1,
            grid=(rows // EXPERT_ROWS,),
            in_specs=[blk,
                      pl.BlockSpec((1, D_MODEL, 2 * D_EXPERT), lambda i, be: (be[i], 0, 0)),
                      pl.BlockSpec((1, D_EXPERT, D_MODEL), lambda i, be: (be[i], 0, 0))],
            out_specs=blk),
        out_shape=jax.ShapeDtypeStruct(x_pad3.shape, F32),
        compiler_params=_params(1),
    )(block_expert, x_pad3, w_up, w_down)


def _combine_kernel(d0_ref, d1_ref, y_hbm, h_ref, gate_ref, o_ref, buf0, buf1, sem, *, tm, lp, pad):
    t = pl.program_id(0)

    def issue(r, carry):
        _gather_copy(y_hbm, buf0, sem.at[0], d0_ref[0, 0, r], r).start()
        _gather_copy(y_hbm, buf1, sem.at[1], d1_ref[0, 0, r], r).start()
        return carry

    lax.fori_loop(0, tm, issue, 0)
    pltpu.make_async_copy(y_hbm.at[pl.ds(0, tm)], buf0, sem.at[0]).wait()
    pltpu.make_async_copy(y_hbm.at[pl.ds(0, tm)], buf1, sem.at[1]).wait()
    g0 = gate_ref[:, 0:1]
    g1 = gate_ref[:, 1:2]
    pos = (t * tm + lax.broadcasted_iota(jnp.int32, (tm, LANES), 0)) % lp
    for s in range(ROW_TILES):
        sl = slice(s * LANES, (s + 1) * LANES)
        y = h_ref[:, sl] + g0 * buf0[:, s, :] + g1 * buf1[:, s, :]
        o_ref[:, sl] = jnp.where(pos >= pad, y, 0.0)


def _combine(y3, dest, gate, h2, lp, pad):
    n = h2.shape[0]
    tm = _row_tile(lp)
    steps = n // tm
    idx = pl.BlockSpec((1, 1, tm), lambda t: (t, 0, 0), memory_space=pltpu.SMEM)
    row = pl.BlockSpec((tm, D_MODEL), lambda t: (t, 0))
    return pl.pallas_call(
        functools.partial(_combine_kernel, tm=tm, lp=lp, pad=pad),
        grid=(steps,),
        in_specs=[idx, idx, pl.BlockSpec(memory_space=pl.ANY), row,
                  pl.BlockSpec((tm, LANES), lambda t: (t, 0))],
        out_specs=row,
        out_shape=jax.ShapeDtypeStruct(h2.shape, F32),
        scratch_shapes=[pltpu.VMEM((tm, ROW_TILES, LANES), F32), pltpu.VMEM((tm, ROW_TILES, LANES), F32),
                        pltpu.SemaphoreType.DMA((2,))],
        compiler_params=_params(1),
    )(dest[:, 0].reshape(steps, 1, tm), dest[:, 1].reshape(steps, 1, tm), y3, h2, gate)


def _moe_layer(h, gain, w_grp, b_grp, w_rt, b_rt, w_up, w_down, pad):
    bsz, lp, _ = h.shape
    n = bsz * lp
    h2 = h.reshape(n, D_MODEL)
    fill = LANES - N_EXPERTS - N_GROUPS
    w_cat = jnp.concatenate([w_rt, w_grp, jnp.zeros((D_MODEL, fill), F32)], axis=1)
    b_cat = jnp.concatenate([b_rt, b_grp, jnp.zeros((fill,), F32)])[None, :]
    idx, gate, xn3 = _router(h2, gain, w_cat, b_cat)

    n_asg = n * TOP_K
    expert = idx[:, :TOP_K].reshape(n_asg)
    order = jnp.argsort(expert)
    se = expert[order]
    counts = jnp.bincount(expert, length=N_EXPERTS)
    padded = (counts + EXPERT_ROWS - 1) // EXPERT_ROWS * EXPERT_ROWS
    start = jnp.cumsum(counts) - counts
    pend = jnp.cumsum(padded)
    pstart = pend - padded
    dest_sorted = (pstart[se] + jnp.arange(n_asg) - start[se]).astype(jnp.int32)
    n_blocks = (n_asg + N_EXPERTS * (EXPERT_ROWS - 1) + EXPERT_ROWS - 1) // EXPERT_ROWS
    rows = n_blocks * EXPERT_ROWS
    src_tok = jnp.zeros((rows,), jnp.int32).at[dest_sorted].set((order // TOP_K).astype(jnp.int32))
    dest = jnp.zeros((n_asg,), jnp.int32).at[order].set(dest_sorted).reshape(n, TOP_K)
    block_expert = jnp.minimum(
        jnp.searchsorted(pend, jnp.arange(n_blocks) * EXPERT_ROWS, side='right'),
        N_EXPERTS - 1).astype(jnp.int32)

    x_pad3 = _gather_rows(xn3, src_tok, EXPERT_ROWS)
    y3 = _experts(x_pad3, block_expert, w_up, w_down)
    return _combine(y3, dest, gate, h2, lp, pad).reshape(bsz, lp, D_MODEL)


def _lower_bounds(lb_logits):
    p = jax.nn.softmax(lb_logits.astype(F32), axis=0)
    cum = jnp.cumsum(p, axis=0)
    return cum - cum[:1]


@jax.jit
def kernel(x, meta_tokens, fox_norm, fox_w_in, fox_b_f, fox_q_norm, fox_k_norm, fox_w_out, hg_norm, hg_w_in, hg_b_f, hg_lb_logits, hg_o_norm, hg_w_out, moe_norm, moe_w_grp, moe_b_grp, moe_w_rt, moe_b_rt, moe_w_up, moe_w_down):
    bsz, seq, _ = x.shape
    depth = moe_norm.shape[0]
    length = N_META + seq
    pad = (-length) % ROW_ALIGN
    meta = jnp.broadcast_to(meta_tokens[None].astype(x.dtype), (bsz, N_META, D_MODEL))
    h = jnp.concatenate([jnp.zeros((bsz, pad, D_MODEL), x.dtype), meta, x], axis=1)
    lower_bounds = _lower_bounds(hg_lb_logits)
    scale = FOX_HEAD_DIM ** -0.5
    for i in range(depth):
        j = i // 2
        if i % 2 == 0:
            q, k, v, cum = _fox_in(
                h, fox_norm[j][None, :], fox_w_in[j][:, :3 * D_MODEL].astype(BF16),
                fox_w_in[j][:, 3 * D_MODEL:], fox_b_f[j][None, :],
                jnp.tile(fox_q_norm[j] * scale, FOX_HEADS)[None, :],
                jnp.tile(fox_k_norm[j], FOX_HEADS)[None, :], pad)
            o = _flash(*_fox_augment(q, k, v, cum, pad))
            o = o[..., :FOX_HEAD_DIM].transpose(0, 2, 1, 3).reshape(h.shape)
            h = _out_proj(o, fox_w_out[j].astype(BF16), h, pad)
        else:
            h = _hgrn_layer(h, hg_norm[j][None, :], hg_w_in[j].astype(BF16), hg_b_f[j][None, :],
                            lower_bounds[i][None, :], hg_o_norm[j][None, :],
                            hg_w_out[j].astype(BF16), pad)
        h = _moe_layer(h, moe_norm[i][None, :], moe_w_grp[i], moe_b_grp[i], moe_w_rt[i], moe_b_rt[i],
                       moe_w_up[i].astype(BF16), moe_w_down[i].astype(BF16), pad)
    return h[:, pad + N_META:]
```

```python
import functools

import jax
import jax.numpy as jnp
from jax import lax
from jax.experimental import pallas as pl
from jax.experimental.pallas import tpu as pltpu

F32 = jnp.float32
BF16 = jnp.bfloat16
HIGHEST = lax.Precision.HIGHEST

D_MODEL = 1024
N_META = 16
ROW_ALIGN = 128
EPS = 1e-6
NEG_BIG = -1e30

FOX_HEADS = 16
FOX_HEAD_DIM = D_MODEL // FOX_HEADS
HG_HEADS = 8
HG_KEY_DIM = 128
HG_VAL_DIM = D_MODEL // HG_HEADS
HG_KEY_TOTAL = HG_HEADS * HG_KEY_DIM
HG_CHUNK = 128
HG_SUB = 16

N_GROUPS = 4
EXPERTS_PER_GROUP = 8
N_EXPERTS = N_GROUPS * EXPERTS_PER_GROUP
TOP_K = 2
D_EXPERT = D_MODEL // 2
EXPERT_ROWS = 256

LANES = 128
SUBLANES = 8
ROW_TILES = D_MODEL // LANES
VMEM_LIMIT = 56 * 1024 * 1024


def _dot(a, b, precision=None):
    return jnp.dot(a, b, preferred_element_type=F32, precision=precision)


def _dot_nt(a, b):
    return lax.dot_general(a, b, (((1,), (1,)), ((), ())), preferred_element_type=F32)


def _dot_tn(a, b):
    return lax.dot_general(a, b, (((0,), (0,)), ((), ())), preferred_element_type=F32)


def _rms(x, gain):
    return x * lax.rsqrt(jnp.mean(x * x, axis=-1, keepdims=True) + EPS) * gain


def _sigmoid(x):
    return 1.0 / (1.0 + jnp.exp(-x))


def _silu(x):
    return x * _sigmoid(x)


def _row_tile(lp):
    for t in (384, 256, 128):
        if lp % t == 0:
            return t
    raise ValueError(lp)


def _params(n_axes):
    return pltpu.CompilerParams(dimension_semantics=("arbitrary",) * n_axes,
                                vmem_limit_bytes=VMEM_LIMIT)


def _full(shape):
    return pl.BlockSpec(shape, lambda *_: (0,) * len(shape))


def _fox_in_kernel(h_ref, gain_ref, wqkv_ref, wf_ref, bf_ref, gq_ref, gk_ref, seg_ref, segt_ref,
                   tri_ref, q_ref, k_ref, v_ref, c_ref, carry_ref, *, tm, pad):
    t = pl.program_id(1)

    @pl.when(t == 0)
    def _():
        carry_ref[...] = jnp.zeros_like(carry_ref)

    xn = _rms(h_ref[0], gain_ref[...])
    qkv = _dot(xn.astype(BF16), wqkv_ref[...])

    def head_norm(z, g_ref):
        ms = _dot(z * z, seg_ref[...], HIGHEST) * (1.0 / FOX_HEAD_DIM)
        r = _dot(lax.rsqrt(ms + EPS), segt_ref[...], HIGHEST)
        return z * r * g_ref[...]

    q_ref[0] = head_norm(qkv[:, :D_MODEL], gq_ref).astype(BF16)
    k_ref[0] = head_norm(qkv[:, D_MODEL:2 * D_MODEL], gk_ref).astype(BF16)
    v_ref[0] = qkv[:, 2 * D_MODEL:].astype(BF16)

    fl = _dot(xn, wf_ref[...], HIGHEST) + bf_ref[...]
    log_f = jnp.minimum(fl, 0.0) - jnp.log(1.0 + jnp.exp(-jnp.abs(fl)))
    pos = t * tm + lax.broadcasted_iota(jnp.int32, log_f.shape, 0)
    log_f = jnp.where(pos >= pad, log_f, 0.0)
    cum = _dot(tri_ref[...], log_f, HIGHEST) + carry_ref[...]
    c_ref[0] = cum
    carry_ref[...] = cum[tm - 1:tm, :]


def _fox_in(h, gain, wqkv, wf, bf, gq, gk, pad):
    bsz, lp, _ = h.shape
    tm = _row_tile(lp)
    head_of = jnp.arange(D_MODEL) // FOX_HEAD_DIM
    seg = (head_of[:, None] == jnp.arange(FOX_HEADS)[None, :]).astype(F32)
    tri = jnp.tril(jnp.ones((tm, tm), F32))
    row = pl.BlockSpec((1, tm, D_MODEL), lambda b, t: (b, t, 0))
    return pl.pallas_call(
        functools.partial(_fox_in_kernel, tm=tm, pad=pad),
        grid=(bsz, lp // tm),
        in_specs=[row, _full((1, D_MODEL)), _full((D_MODEL, 3 * D_MODEL)),
                  _full((D_MODEL, FOX_HEADS)), _full((1, FOX_HEADS)),
                  _full((1, D_MODEL)), _full((1, D_MODEL)),
                  _full((D_MODEL, FOX_HEADS)), _full((FOX_HEADS, D_MODEL)), _full((tm, tm))],
        out_specs=[row, row, row, pl.BlockSpec((1, tm, FOX_HEADS), lambda b, t: (b, t, 0))],
        out_shape=[jax.ShapeDtypeStruct((bsz, lp, D_MODEL), BF16)] * 3
        + [jax.ShapeDtypeStruct((bsz, lp, FOX_HEADS), F32)],
        scratch_shapes=[pltpu.VMEM((1, FOX_HEADS), F32)],
        compiler_params=_params(2),
    )(h, gain, wqkv, wf, bf, gq, gk, seg, seg.T, tri)


ATT_BLOCK = 256
ATT_Q_STRIPS = 5
LOG2E = 1.4426950408889634


def _flash_kernel(q_ref, k_ref, v_ref, o_ref, acc_ref, *, tq, tk):
    i = pl.program_id(2)
    n_strip = tq // tk
    lane = lax.broadcasted_iota(jnp.int32, (tq, LANES), 1)

    for hd in range(2):
        q = q_ref[0, hd]

        def strip(rows, off, masked):
            s = _dot_nt(q[rows:], k_ref[0, hd, pl.ds(off, tk), :])
            if masked:
                r = lax.broadcasted_iota(jnp.int32, s.shape, 0)
                c = lax.broadcasted_iota(jnp.int32, s.shape, 1)
                s = jnp.where(c <= r, s, NEG_BIG)
            p = jnp.exp2(s).astype(BF16)
            return _dot(p, v_ref[0, hd, pl.ds(off, tk), :])

        def below_diagonal(j, carry):
            total = None
            for n in range(n_strip):
                part = strip(0, pl.multiple_of(j * tq + n * tk, tk), False)
                total = part if total is None else total + part
            acc_ref[...] += total
            return carry

        acc_ref[...] = jnp.zeros_like(acc_ref)
        lax.fori_loop(0, i, below_diagonal, 0)
        for n in range(n_strip):
            acc_ref[n * tk:, :] += strip(n * tk, pl.multiple_of(i * tq + n * tk, tk), True)
        acc = acc_ref[...]
        denom = acc[:, FOX_HEAD_DIM:FOX_HEAD_DIM + 1]
        out = acc / jnp.where(denom > 0.0, denom, 1.0)
        if hd == 0:
            o_ref[0] = out.astype(o_ref.dtype)
        else:
            shifted = pltpu.roll(out, FOX_HEAD_DIM, axis=1).astype(o_ref.dtype)
            o_ref[0] = jnp.where(lane < FOX_HEAD_DIM, o_ref[0], shifted)


def _flash(qa, ka, va, lp):
    bsz, nh, lpa, _ = qa.shape
    tk = ATT_BLOCK
    tq = ATT_Q_STRIPS * tk if lpa % (ATT_Q_STRIPS * tk) == 0 else tk
    qspec = pl.BlockSpec((1, 2, tq, LANES), lambda b, h, i: (b, h, i, 0))
    kvspec = pl.BlockSpec((1, 2, lpa, LANES), lambda b, h, i: (b, h, 0, 0))
    return pl.pallas_call(
        functools.partial(_flash_kernel, tq=tq, tk=tk),
        grid=(bsz, nh // 2, lpa // tq),
        in_specs=[qspec, kvspec, kvspec],
        out_specs=pl.BlockSpec((1, tq, LANES), lambda b, h, i: (b, i, h)),
        out_shape=jax.ShapeDtypeStruct((bsz, lp, D_MODEL), BF16),
        scratch_shapes=[pltpu.VMEM((tq, LANES), F32)],
        compiler_params=_params(3),
        name="fox_flash",
    )(qa, ka, va)


def _bf16_trunc(x):
    bits = lax.bitcast_convert_type(x, jnp.uint32) & jnp.uint32(0xFFFF0000)
    return lax.bitcast_convert_type(bits, F32)


def _split3(c):
    hi = _bf16_trunc(c)
    mid = _bf16_trunc(c - hi)
    lo = c - hi - mid
    return hi.astype(BF16), mid.astype(BF16), lo.astype(BF16)


def _fox_augment(q, k, v, cum, shift, pad):
    bsz, lp, _ = q.shape
    lpa = -(-lp // ATT_BLOCK) * ATT_BLOCK

    def heads(t):
        t = t.reshape(bsz, lp, FOX_HEADS, FOX_HEAD_DIM).transpose(0, 2, 1, 3)
        return jnp.pad(t, ((0, 0), (0, 0), (0, lpa - lp), (0, 0)))

    def col(t):
        return jnp.pad(t.transpose(0, 2, 1), ((0, 0), (0, 0), (0, lpa - lp)))[..., None]

    hi, mid, lo = (col(t) for t in _split3(cum * LOG2E))
    one = col(jnp.ones(cum.shape, BF16))
    neg_shift = -(shift * LOG2E).astype(BF16) * one
    key_pad = one * jnp.where(jnp.arange(lpa) < pad, NEG_BIG, 0.0).astype(BF16)[None, None, :, None]
    fill = jnp.zeros((bsz, FOX_HEADS, lpa, LANES - FOX_HEAD_DIM - 8), BF16)
    qa = jnp.concatenate([heads(q), hi, mid, lo, one, one, one, one, neg_shift, fill], axis=-1)
    ka = jnp.concatenate([heads(k), one, one, one, -hi, -mid, -lo, key_pad, one, fill], axis=-1)
    fill_v = jnp.zeros((bsz, FOX_HEADS, lpa, LANES - FOX_HEAD_DIM - 1), BF16)
    va = jnp.concatenate([heads(v), one, fill_v], axis=-1)
    return qa, ka, va


def _out_proj_kernel(a_ref, w_ref, h_ref, o_ref, *, tm, pad):
    t = pl.program_id(1)
    y = h_ref[0] + _dot(a_ref[0], w_ref[...])
    pos = t * tm + lax.broadcasted_iota(jnp.int32, y.shape, 0)
    o_ref[0] = jnp.where(pos >= pad, y, 0.0)


def _out_proj(a, w, h, pad):
    bsz, lp, _ = h.shape
    tm = _row_tile(lp)
    row = pl.BlockSpec((1, tm, D_MODEL), lambda b, t: (b, t, 0))
    return pl.pallas_call(
        functools.partial(_out_proj_kernel, tm=tm, pad=pad),
        grid=(bsz, lp // tm),
        in_specs=[row, _full((D_MODEL, D_MODEL)), row],
        out_specs=row,
        out_shape=jax.ShapeDtypeStruct(h.shape, F32),
        compiler_params=_params(2),
    )(a, w, h)


def _hgrn_kernel(h_ref, gain_ref, win_ref, bf_ref, lb_ref, go_ref, wout_ref, tri_ref, o_ref,
                 state_ref, *, pad):
    t = pl.program_id(1)

    @pl.when(t == 0)
    def _():
        state_ref[...] = jnp.zeros_like(state_ref)

    x = h_ref[0]
    proj = _dot(_rms(x, gain_ref[...]).astype(BF16), win_ref[...])
    tri = tri_ref[...]
    row = lax.broadcasted_iota(jnp.int32, (HG_CHUNK, HG_KEY_DIM), 0)
    causal = (lax.broadcasted_iota(jnp.int32, (HG_CHUNK, HG_CHUNK), 1)
              <= lax.broadcasted_iota(jnp.int32, (HG_CHUNK, HG_CHUNK), 0))
    outs = []
    for hd in range(HG_HEADS):
        ksl = slice(hd * HG_KEY_DIM, (hd + 1) * HG_KEY_DIM)
        vsl = slice(hd * HG_VAL_DIM, (hd + 1) * HG_VAL_DIM)
        q = _silu(proj[:, ksl])
        lb = lb_ref[:, ksl]
        f = lb + (1.0 - lb) * _sigmoid(proj[:, HG_KEY_TOTAL:2 * HG_KEY_TOTAL][:, ksl] + bf_ref[:, ksl])
        log_f = jnp.log(f)
        kk = 1.0 - f
        v = proj[:, 2 * HG_KEY_TOTAL:2 * HG_KEY_TOTAL + D_MODEL][:, vsl]
        lf_hi = log_f.astype(BF16)
        lf_lo = (log_f - lf_hi.astype(F32)).astype(BF16)
        b = _dot(tri, lf_hi) + _dot(tri, lf_lo)
        state = state_ref[hd]
        o = _dot_nt(q * jnp.exp(b), state)
        blocks = []
        for sc in range(HG_CHUNK // HG_SUB):
            lo_r, hi_r = sc * HG_SUB, (sc + 1) * HG_SUB
            ref = b[lo_r - 1:lo_r, :] if sc else jnp.zeros((1, HG_KEY_DIM), F32)
            qt = q[lo_r:hi_r] * jnp.exp(b[lo_r:hi_r] - ref)
            kt = kk * jnp.exp(jnp.where(row < hi_r, ref - b, NEG_BIG))
            blocks.append(_dot_nt(qt, kt))
        scores = jnp.where(causal, jnp.concatenate(blocks, axis=0), 0.0)
        o = o + _dot(scores, v)
        b_end = b[HG_CHUNK - 1:HG_CHUNK, :]
        state_ref[hd] = jnp.exp(b_end) * state + _dot_tn(v, kk * jnp.exp(b_end - b))
        outs.append(_rms(o, go_ref[...]))
    gate = _silu(proj[:, 2 * HG_KEY_TOTAL + D_MODEL:])
    y = x + _dot((jnp.concatenate(outs, axis=-1) * gate).astype(BF16), wout_ref[...])
    pos = t * HG_CHUNK + lax.broadcasted_iota(jnp.int32, y.shape, 0)
    o_ref[0] = jnp.where(pos >= pad, y, 0.0)


def _hgrn_layer(h, gain, win, bf, lb, go, wout, pad):
    bsz, lp, _ = h.shape
    hg_in = win.shape[1]
    tri = jnp.tril(jnp.ones((HG_CHUNK, HG_CHUNK), BF16))
    row = pl.BlockSpec((1, HG_CHUNK, D_MODEL), lambda b, t: (b, t, 0))
    return pl.pallas_call(
        functools.partial(_hgrn_kernel, pad=pad),
        grid=(bsz, lp // HG_CHUNK),
        in_specs=[row, _full((1, D_MODEL)), _full((D_MODEL, hg_in)), _full((1, HG_KEY_TOTAL)),
                  _full((1, HG_KEY_TOTAL)), _full((1, HG_VAL_DIM)), _full((D_MODEL, D_MODEL)),
                  _full((HG_CHUNK, HG_CHUNK))],
        out_specs=row,
        out_shape=jax.ShapeDtypeStruct(h.shape, F32),
        scratch_shapes=[pltpu.VMEM((HG_HEADS, HG_VAL_DIM, HG_KEY_DIM), F32)],
        compiler_params=_params(2),
    )(h, gain, win, bf, lb, go, wout, tri)


def _router_kernel(h_ref, gain_ref, w_ref, b_ref, idx_ref, gate_ref, xn_ref):
    xn = _rms(h_ref[...], gain_ref[...])
    for s in range(ROW_TILES):
        xn_ref[:, s, :] = xn[:, s * LANES:(s + 1) * LANES]
    logit = _dot(xn, w_ref[...], HIGHEST) + b_ref[...]
    lane = lax.broadcasted_iota(jnp.int32, logit.shape, 1)
    is_grp = (lane >= N_EXPERTS) & (lane < N_EXPERTS + N_GROUPS)
    g_logit = jnp.where(is_grp, logit, -jnp.inf)
    g_max = jnp.max(g_logit, axis=-1, keepdims=True)
    g_idx = jnp.min(jnp.where(g_logit == g_max, lane, 4 * LANES), axis=-1, keepdims=True) - N_EXPERTS
    g_gate = 1.0 / jnp.sum(jnp.exp(g_logit - g_max), axis=-1, keepdims=True)
    in_grp = (lane >= g_idx * EXPERTS_PER_GROUP) & (lane < (g_idx + 1) * EXPERTS_PER_GROUP)
    e1 = jnp.where(in_grp, logit, -jnp.inf)
    v1 = jnp.max(e1, axis=-1, keepdims=True)
    i1 = jnp.min(jnp.where(e1 == v1, lane, 4 * LANES), axis=-1, keepdims=True)
    e2 = jnp.where(lane == i1, -jnp.inf, e1)
    v2 = jnp.max(e2, axis=-1, keepdims=True)
    i2 = jnp.min(jnp.where(e2 == v2, lane, 4 * LANES), axis=-1, keepdims=True)
    z = jnp.exp(v2 - v1)
    w1 = g_gate / (1.0 + z)
    idx_ref[...] = jnp.where(lane == 0, i1, jnp.where(lane == 1, i2, 0))
    gate_ref[...] = jnp.where(lane == 0, w1, jnp.where(lane == 1, w1 * z, 0.0))


def _router(h2, gain, w_cat, b_cat):
    n = h2.shape[0]
    tm = _row_tile(n)
    row = pl.BlockSpec((tm, D_MODEL), lambda t: (t, 0))
    lanes = pl.BlockSpec((tm, LANES), lambda t: (t, 0))
    return pl.pallas_call(
        _router_kernel,
        grid=(n // tm,),
        in_specs=[row, _full((1, D_MODEL)), _full((D_MODEL, LANES)), _full((1, LANES))],
        out_specs=[lanes, lanes, pl.BlockSpec((tm, ROW_TILES, LANES), lambda t: (t, 0, 0))],
        out_shape=[jax.ShapeDtypeStruct((n, LANES), jnp.int32),
                   jax.ShapeDtypeStruct((n, LANES), F32),
                   jax.ShapeDtypeStruct((n, ROW_TILES, LANES), F32)],
        compiler_params=_params(1),
    )(h2, gain, w_cat, b_cat)


def _gather_copy(src_hbm, dst_ref, sem, src_row, dst_row):
    return pltpu.make_async_copy(src_hbm.at[pl.ds(src_row, 1)], dst_ref.at[pl.ds(dst_row, 1)], sem)


def _gather_kernel(idx_ref, src_hbm, o_ref, sem, *, rows):
    def issue(r, carry):
        _gather_copy(src_hbm, o_ref, sem, idx_ref[0, 0, r], r).start()
        return carry

    lax.fori_loop(0, rows, issue, 0)
    pltpu.make_async_copy(src_hbm.at[pl.ds(0, rows)], o_ref, sem).wait()


def _gather_rows(src3, idx, rows_per_step):
    n_out = idx.shape[0]
    steps = n_out // rows_per_step
    return pl.pallas_call(
        functools.partial(_gather_kernel, rows=rows_per_step),
        grid=(steps,),
        in_specs=[pl.BlockSpec((1, 1, rows_per_step), lambda i: (i, 0, 0), memory_space=pltpu.SMEM),
                  pl.BlockSpec(memory_space=pl.ANY)],
        out_specs=pl.BlockSpec((rows_per_step, ROW_TILES, LANES), lambda i: (i, 0, 0)),
        out_shape=jax.ShapeDtypeStruct((n_out, ROW_TILES, LANES), F32),
        scratch_shapes=[pltpu.SemaphoreType.DMA(())],
        compiler_params=_params(1),
    )(idx.reshape(steps, 1, rows_per_step), src3)


def _expert_kernel(be_ref, x_ref, wup_ref, wdn_ref, y_ref):
    del be_ref
    x = jnp.concatenate([x_ref[:, s, :] for s in range(ROW_TILES)], axis=-1)
    gu = _dot(x.astype(BF16), wup_ref[0])
    act = _silu(gu[:, :D_EXPERT]) * gu[:, D_EXPERT:]
    y = _dot(act.astype(BF16), wdn_ref[0])
    for s in range(ROW_TILES):
        y_ref[:, s, :] = y[:, s * LANES:(s + 1) * LANES]


def _experts(x_pad3, block_expert, w_up, w_down):
    rows = x_pad3.shape[0]
    blk = pl.BlockSpec((EXPERT_ROWS, ROW_TILES, LANES), lambda i, be: (i, 0, 0))
    return pl.pallas_call(
        _expert_kernel,
        grid_spec=pltpu.PrefetchScalarGridSpec(
            num_scalar_prefetch=1,
            grid=(rows // EXPERT_ROWS,),
            in_specs=[blk,
                      pl.BlockSpec((1, D_MODEL, 2 * D_EXPERT), lambda i, be: (be[i], 0, 0)),
                      pl.BlockSpec((1, D_EXPERT, D_MODEL), lambda i, be: (be[i], 0, 0))],
            out_specs=blk),
        out_shape=jax.ShapeDtypeStruct(x_pad3.shape, F32),
        compiler_params=_params(1),
    )(block_expert, x_pad3, w_up, w_down)


def _combine_kernel(d0_ref, d1_ref, y_hbm, h_ref, gate_ref, o_ref, buf0, buf1, sem, *, tm, lp, pad):
    t = pl.program_id(0)

    def issue(r, carry):
        _gather_copy(y_hbm, buf0, sem.at[0], d0_ref[0, 0, r], r).start()
        _gather_copy(y_hbm, buf1, sem.at[1], d1_ref[0, 0, r], r).start()
        return carry

    lax.fori_loop(0, tm, issue, 0)
    pltpu.make_async_copy(y_hbm.at[pl.ds(0, tm)], buf0, sem.at[0]).wait()
    pltpu.make_async_copy(y_hbm.at[pl.ds(0, tm)], buf1, sem.at[1]).wait()
    g0 = gate_ref[:, 0:1]
    g1 = gate_ref[:, 1:2]
    pos = (t * tm + lax.broadcasted_iota(jnp.int32, (tm, LANES), 0)) % lp
    for s in range(ROW_TILES):
        sl = slice(s * LANES, (s + 1) * LANES)
        y = h_ref[:, sl] + g0 * buf0[:, s, :] + g1 * buf1[:, s, :]
        o_ref[:, sl] = jnp.where(pos >= pad, y, 0.0)


def _combine(y3, dest, gate, h2, lp, pad):
    n = h2.shape[0]
    tm = _row_tile(lp)
    steps = n // tm
    idx = pl.BlockSpec((1, 1, tm), lambda t: (t, 0, 0), memory_space=pltpu.SMEM)
    row = pl.BlockSpec((tm, D_MODEL), lambda t: (t, 0))
    return pl.pallas_call(
        functools.partial(_combine_kernel, tm=tm, lp=lp, pad=pad),
        grid=(steps,),
        in_specs=[idx, idx, pl.BlockSpec(memory_space=pl.ANY), row,
                  pl.BlockSpec((tm, LANES), lambda t: (t, 0))],
        out_specs=row,
        out_shape=jax.ShapeDtypeStruct(h2.shape, F32),
        scratch_shapes=[pltpu.VMEM((tm, ROW_TILES, LANES), F32), pltpu.VMEM((tm, ROW_TILES, LANES), F32),
                        pltpu.SemaphoreType.DMA((2,))],
        compiler_params=_params(1),
    )(dest[:, 0].reshape(steps, 1, tm), dest[:, 1].reshape(steps, 1, tm), y3, h2, gate)


def _moe_layer(h, gain, w_grp, b_grp, w_rt, b_rt, w_up, w_down, pad):
    bsz, lp, _ = h.shape
    n = bsz * lp
    h2 = h.reshape(n, D_MODEL)
    fill = LANES - N_EXPERTS - N_GROUPS
    w_cat = jnp.concatenate([w_rt, w_grp, jnp.zeros((D_MODEL, fill), F32)], axis=1)
    b_cat = jnp.concatenate([b_rt, b_grp, jnp.zeros((fill,), F32)])[None, :]
    idx, gate, xn3 = _router(h2, gain, w_cat, b_cat)

    n_asg = n * TOP_K
    expert = idx[:, :TOP_K].reshape(n_asg)
    order = jnp.argsort(expert)
    se = expert[order]
    counts = jnp.bincount(expert, length=N_EXPERTS)
    padded = (counts + EXPERT_ROWS - 1) // EXPERT_ROWS * EXPERT_ROWS
    start = jnp.cumsum(counts) - counts
    pend = jnp.cumsum(padded)
    pstart = pend - padded
    dest_sorted = (pstart[se] + jnp.arange(n_asg) - start[se]).astype(jnp.int32)
    n_blocks = (n_asg + N_EXPERTS * (EXPERT_ROWS - 1) + EXPERT_ROWS - 1) // EXPERT_ROWS
    rows = n_blocks * EXPERT_ROWS
    src_tok = jnp.zeros((rows,), jnp.int32).at[dest_sorted].set((order // TOP_K).astype(jnp.int32))
    dest = jnp.zeros((n_asg,), jnp.int32).at[order].set(dest_sorted).reshape(n, TOP_K)
    block_expert = jnp.minimum(
        jnp.searchsorted(pend, jnp.arange(n_blocks) * EXPERT_ROWS, side='right'),
        N_EXPERTS - 1).astype(jnp.int32)

    x_pad3 = _gather_rows(xn3, src_tok, EXPERT_ROWS)
    y3 = _experts(x_pad3, block_expert, w_up, w_down)
    return _combine(y3, dest, gate, h2, lp, pad).reshape(bsz, lp, D_MODEL)


def _lower_bounds(lb_logits):
    p = jax.nn.softmax(lb_logits.astype(F32), axis=0)
    cum = jnp.cumsum(p, axis=0)
    return cum - cum[:1]


@jax.jit
def kernel(x, meta_tokens, fox_norm, fox_w_in, fox_b_f, fox_q_norm, fox_k_norm, fox_w_out, hg_norm, hg_w_in, hg_b_f, hg_lb_logits, hg_o_norm, hg_w_out, moe_norm, moe_w_grp, moe_b_grp, moe_w_rt, moe_b_rt, moe_w_up, moe_w_down):
    bsz, seq, _ = x.shape
    depth = moe_norm.shape[0]
    length = N_META + seq
    pad = (-length) % ROW_ALIGN
    meta = jnp.broadcast_to(meta_tokens[None].astype(x.dtype), (bsz, N_META, D_MODEL))
    h = jnp.concatenate([jnp.zeros((bsz, pad, D_MODEL), x.dtype), meta, x], axis=1)
    lower_bounds = _lower_bounds(hg_lb_logits)
    scale = FOX_HEAD_DIM ** -0.5
    for i in range(depth):
        j = i // 2
        if i % 2 == 0:
            q, k, v, cum = _fox_in(
                h, fox_norm[j][None, :], fox_w_in[j][:, :3 * D_MODEL].astype(BF16),
                fox_w_in[j][:, 3 * D_MODEL:], fox_b_f[j][None, :],
                jnp.tile(fox_q_norm[j] * (scale * LOG2E), FOX_HEADS)[None, :],
                jnp.tile(fox_k_norm[j], FOX_HEADS)[None, :], pad)
            shift = FOX_HEAD_DIM * scale * jnp.max(jnp.abs(fox_q_norm[j])) * jnp.max(jnp.abs(fox_k_norm[j]))
            o = _flash(*_fox_augment(q, k, v, cum, shift, pad), h.shape[1])
            h = _out_proj(o, fox_w_out[j].astype(BF16), h, pad)
        else:
            h = _hgrn_layer(h, hg_norm[j][None, :], hg_w_in[j].astype(BF16), hg_b_f[j][None, :],
                            lower_bounds[i][None, :], hg_o_norm[j][None, :],
                            hg_w_out[j].astype(BF16), pad)
        h = _moe_layer(h, moe_norm[i][None, :], moe_w_grp[i], moe_b_grp[i], moe_w_rt[i], moe_b_rt[i],
                       moe_w_up[i].astype(BF16), moe_w_down[i].astype(BF16), pad)
    return h[:, pad + N_META:]
```

```python
import functools

import jax
import jax.numpy as jnp
from jax import lax
from jax.experimental import pallas as pl
from jax.experimental.pallas import tpu as pltpu

F32 = jnp.float32
BF16 = jnp.bfloat16
HIGHEST = lax.Precision.HIGHEST

D_MODEL = 1024
N_META = 16
ROW_ALIGN = 128
EPS = 1e-6
NEG_BIG = -1e30

FOX_HEADS = 16
FOX_HEAD_DIM = D_MODEL // FOX_HEADS
HG_HEADS = 8
HG_KEY_DIM = 128
HG_VAL_DIM = D_MODEL // HG_HEADS
HG_KEY_TOTAL = HG_HEADS * HG_KEY_DIM
HG_CHUNK = 128
HG_SUB = 16

N_GROUPS = 4
EXPERTS_PER_GROUP = 8
N_EXPERTS = N_GROUPS * EXPERTS_PER_GROUP
TOP_K = 2
D_EXPERT = D_MODEL // 2
EXPERT_ROWS = 256

LANES = 128
SUBLANES = 8
ROW_TILES = D_MODEL // LANES
VMEM_LIMIT = 56 * 1024 * 1024


def _dot(a, b, precision=None):
    return jnp.dot(a, b, preferred_element_type=F32, precision=precision)


def _dot_nt(a, b):
    return lax.dot_general(a, b, (((1,), (1,)), ((), ())), preferred_element_type=F32)


def _dot_tn(a, b):
    return lax.dot_general(a, b, (((0,), (0,)), ((), ())), preferred_element_type=F32)


def _rms(x, gain):
    return x * lax.rsqrt(jnp.mean(x * x, axis=-1, keepdims=True) + EPS) * gain


def _sigmoid(x):
    return 1.0 / (1.0 + jnp.exp(-x))


def _silu(x):
    return x * _sigmoid(x)


def _row_tile(lp):
    for t in (384, 256, 128):
        if lp % t == 0:
            return t
    raise ValueError(lp)


def _params(n_axes):
    return pltpu.CompilerParams(dimension_semantics=("arbitrary",) * n_axes,
                                vmem_limit_bytes=VMEM_LIMIT)


def _full(shape):
    return pl.BlockSpec(shape, lambda *_: (0,) * len(shape))


ATT_BLOCK = 256
LOG2E = 1.4426950408889634

LANE_C_Q = FOX_HEAD_DIM
LANE_C_K = FOX_HEAD_DIM + 3
LANE_KEY_PAD = FOX_HEAD_DIM + 6
LANE_SHIFT = FOX_HEAD_DIM + 7
LANE_ONE_V = FOX_HEAD_DIM


def _bf16_trunc(x):
    bits = lax.bitcast_convert_type(x, jnp.int32) & jnp.int32(-65536)
    return lax.bitcast_convert_type(bits, F32)


def _split3(c):
    hi = _bf16_trunc(c)
    mid = _bf16_trunc(c - hi)
    return hi, mid, c - hi - mid


def _split2(x):
    hi = x.astype(BF16)
    return hi, (x - hi.astype(F32)).astype(BF16)


def _fox_in_kernel(h_ref, gain_ref, wqkv_ref, wf_ref, bf_ref, gq_ref, gk_ref, seg_ref, segt_ref,
                   tri_ref, rowq_ref, rowk_ref, qa_ref, ka_ref, va_ref, carry_ref, *, tm, nt, pad):
    t = pl.program_id(1)

    @pl.when(t == 0)
    def _():
        carry_ref[...] = jnp.zeros_like(carry_ref)

    @pl.when(t >= nt)
    def _():
        qa_ref[...] = jnp.zeros_like(qa_ref)
        ka_ref[...] = jnp.zeros_like(ka_ref)
        va_ref[...] = jnp.zeros_like(va_ref)

    @pl.when(t < nt)
    def _():
        xn = _rms(h_ref[0], gain_ref[...])
        qkv = _dot(xn.astype(BF16), wqkv_ref[...])

        def head_norm(z, g_ref):
            hi, lo = _split2(z * z)
            ms = (_dot(hi, seg_ref[...]) + _dot(lo, seg_ref[...])) * (1.0 / FOX_HEAD_DIM)
            hi, lo = _split2(lax.rsqrt(ms + EPS))
            return z * (_dot(hi, segt_ref[...]) + _dot(lo, segt_ref[...])) * g_ref[...]

        qn = head_norm(qkv[:, :D_MODEL], gq_ref)
        kn = head_norm(qkv[:, D_MODEL:2 * D_MODEL], gk_ref)
        vv = qkv[:, 2 * D_MODEL:]

        x_hi, x_lo = _split2(xn)
        w_hi, w_lo = _split2(wf_ref[...])
        fl = _dot(x_hi, w_hi) + _dot(x_hi, w_lo) + _dot(x_lo, w_hi) + bf_ref[...]
        log_f = jnp.minimum(fl, 0.0) - jnp.log(1.0 + jnp.exp(-jnp.abs(fl)))
        pos16 = t * tm + lax.broadcasted_iota(jnp.int32, log_f.shape, 0)
        log_f = jnp.where(pos16 >= pad, log_f, 0.0)
        tri = tri_ref[...]
        cum = carry_ref[...] + sum(_dot(tri, piece.astype(BF16)) for piece in _split3(log_f))
        carry_ref[...] = cum[tm - 1:tm, :]
        c_hi, c_mid, c_lo = _split3(cum * LOG2E)

        lane = lax.broadcasted_iota(jnp.int32, (tm, LANES), 1)
        pos = t * tm + lax.broadcasted_iota(jnp.int32, (tm, LANES), 0)
        is_head = lane < FOX_HEAD_DIM
        row_k = jnp.where((lane == LANE_KEY_PAD) & (pos < pad), NEG_BIG, rowk_ref[...])
        row_v = jnp.where(lane == LANE_ONE_V, 1.0, 0.0)
        for hd in range(FOX_HEADS):
            col = slice((hd // 2) * LANES, (hd // 2 + 1) * LANES)

            def head(z):
                z = z[:, col]
                return pltpu.roll(z, FOX_HEAD_DIM, axis=1) if hd % 2 else z

            def spread(piece):
                return jnp.broadcast_to(piece[:, hd:hd + 1], (tm, LANES))

            hi, mid, lo = spread(c_hi), spread(c_mid), spread(c_lo)
            aug_q = jnp.where(lane == LANE_C_Q, hi, jnp.where(lane == LANE_C_Q + 1, mid,
                              jnp.where(lane == LANE_C_Q + 2, lo, rowq_ref[...])))
            aug_k = jnp.where(lane == LANE_C_K, -hi, jnp.where(lane == LANE_C_K + 1, -mid,
                              jnp.where(lane == LANE_C_K + 2, -lo, row_k)))
            qa_ref[0, hd] = jnp.where(is_head, head(qn), aug_q).astype(BF16)
            ka_ref[0, hd] = jnp.where(is_head, head(kn), aug_k).astype(BF16)
            va_ref[0, hd] = jnp.where(is_head, head(vv), row_v).astype(BF16)


def _fox_in(h, gain, wqkv, wf, bf, gq, gk, shift, pad):
    bsz, lp, _ = h.shape
    tm = _row_tile(lp)
    nt = lp // tm
    lpa = -(-lp // ATT_BLOCK) * ATT_BLOCK
    head_of = jnp.arange(D_MODEL) // FOX_HEAD_DIM
    seg = (head_of[:, None] == jnp.arange(FOX_HEADS)[None, :]).astype(BF16)
    tri = jnp.tril(jnp.ones((tm, tm), BF16))
    lane = jnp.arange(LANES)
    row_q = jnp.where((lane >= LANE_C_K) & (lane <= LANE_KEY_PAD), 1.0,
                      jnp.where(lane == LANE_SHIFT, -shift * LOG2E, 0.0)).astype(F32)[None, :]
    row_k = (((lane >= LANE_C_Q) & (lane < LANE_C_K)) | (lane == LANE_SHIFT)).astype(F32)[None, :]
    out = pl.BlockSpec((1, FOX_HEADS, tm, LANES), lambda b, t: (b, 0, t, 0))
    return pl.pallas_call(
        functools.partial(_fox_in_kernel, tm=tm, nt=nt, pad=pad),
        grid=(bsz, -(-lpa // tm)),
        in_specs=[pl.BlockSpec((1, tm, D_MODEL), lambda b, t: (b, jnp.minimum(t, nt - 1), 0)),
                  _full((1, D_MODEL)), _full((D_MODEL, 3 * D_MODEL)),
                  _full((D_MODEL, FOX_HEADS)), _full((1, FOX_HEADS)),
                  _full((1, D_MODEL)), _full((1, D_MODEL)),
                  _full((D_MODEL, FOX_HEADS)), _full((FOX_HEADS, D_MODEL)), _full((tm, tm)),
                  _full((1, LANES)), _full((1, LANES))],
        out_specs=[out, out, out],
        out_shape=[jax.ShapeDtypeStruct((bsz, FOX_HEADS, lpa, LANES), BF16)] * 3,
        scratch_shapes=[pltpu.VMEM((1, FOX_HEADS), F32)],
        compiler_params=_params(2),
        name="fox_in",
    )(h, gain, wqkv, wf, bf, gq, gk, seg, seg.T, tri, row_q, row_k)


ATT_Q_STRIPS = 5


def _flash_kernel(q_ref, k_ref, v_ref, o_ref, acc_ref, *, tq, tk):
    i = pl.program_id(2)
    n_strip = tq // tk
    lane = lax.broadcasted_iota(jnp.int32, (tq, LANES), 1)

    for hd in range(2):
        q = q_ref[0, hd]

        def strip(rows, off, masked):
            s = _dot_nt(q[rows:], k_ref[0, hd, pl.ds(off, tk), :])
            if masked:
                r = lax.broadcasted_iota(jnp.int32, s.shape, 0)
                c = lax.broadcasted_iota(jnp.int32, s.shape, 1)
                s = jnp.where(c <= r, s, NEG_BIG)
            p = jnp.exp2(s).astype(BF16)
            return _dot(p, v_ref[0, hd, pl.ds(off, tk), :])

        def below_diagonal(j, carry):
            total = None
            for n in range(n_strip):
                part = strip(0, pl.multiple_of(j * tq + n * tk, tk), False)
                total = part if total is None else total + part
            acc_ref[...] += total
            return carry

        acc_ref[...] = jnp.zeros_like(acc_ref)
        lax.fori_loop(0, i, below_diagonal, 0)
        for n in range(n_strip):
            acc_ref[n * tk:, :] += strip(n * tk, pl.multiple_of(i * tq + n * tk, tk), True)
        acc = acc_ref[...]
        denom = acc[:, FOX_HEAD_DIM:FOX_HEAD_DIM + 1]
        out = acc / jnp.where(denom > 0.0, denom, 1.0)
        if hd == 0:
            o_ref[0] = out.astype(o_ref.dtype)
        else:
            shifted = pltpu.roll(out, FOX_HEAD_DIM, axis=1).astype(o_ref.dtype)
            o_ref[0] = jnp.where(lane < FOX_HEAD_DIM, o_ref[0], shifted)


def _flash(qa, ka, va, lp):
    bsz, nh, lpa, _ = qa.shape
    tk = ATT_BLOCK
    tq = ATT_Q_STRIPS * tk if lpa % (ATT_Q_STRIPS * tk) == 0 else tk
    qspec = pl.BlockSpec((1, 2, tq, LANES), lambda b, h, i: (b, h, i, 0))
    kvspec = pl.BlockSpec((1, 2, lpa, LANES), lambda b, h, i: (b, h, 0, 0))
    return pl.pallas_call(
        functools.partial(_flash_kernel, tq=tq, tk=tk),
        grid=(bsz, nh // 2, lpa // tq),
        in_specs=[qspec, kvspec, kvspec],
        out_specs=pl.BlockSpec((1, tq, LANES), lambda b, h, i: (b, i, h)),
        out_shape=jax.ShapeDtypeStruct((bsz, lp, D_MODEL), BF16),
        scratch_shapes=[pltpu.VMEM((tq, LANES), F32)],
        compiler_params=_params(3),
        name="fox_flash",
    )(qa, ka, va)


def _out_proj_kernel(a_ref, w_ref, h_ref, o_ref, *, tm, pad):
    t = pl.program_id(1)
    y = h_ref[0] + _dot(a_ref[0], w_ref[...])
    pos = t * tm + lax.broadcasted_iota(jnp.int32, y.shape, 0)
    o_ref[0] = jnp.where(pos >= pad, y, 0.0)


def _out_proj(a, w, h, pad):
    bsz, lp, _ = h.shape
    tm = _row_tile(lp)
    row = pl.BlockSpec((1, tm, D_MODEL), lambda b, t: (b, t, 0))
    return pl.pallas_call(
        functools.partial(_out_proj_kernel, tm=tm, pad=pad),
        grid=(bsz, lp // tm),
        in_specs=[row, _full((D_MODEL, D_MODEL)), row],
        out_specs=row,
        out_shape=jax.ShapeDtypeStruct(h.shape, F32),
        compiler_params=_params(2),
        name="out_proj",
    )(a, w, h)


def _hgrn_kernel(h_ref, gain_ref, win_ref, bf_ref, lb_ref, go_ref, wout_ref, tri_ref, o_ref,
                 state_ref, *, pad):
    t = pl.program_id(1)

    @pl.when(t == 0)
    def _():
        state_ref[...] = jnp.zeros_like(state_ref)

    x = h_ref[0]
    proj = _dot(_rms(x, gain_ref[...]).astype(BF16), win_ref[...])
    tri = tri_ref[...]
    row = lax.broadcasted_iota(jnp.int32, (HG_CHUNK, HG_KEY_DIM), 0)
    causal = (lax.broadcasted_iota(jnp.int32, (HG_CHUNK, HG_CHUNK), 1)
              <= lax.broadcasted_iota(jnp.int32, (HG_CHUNK, HG_CHUNK), 0))
    outs = []
    for hd in range(HG_HEADS):
        ksl = slice(hd * HG_KEY_DIM, (hd + 1) * HG_KEY_DIM)
        vsl = slice(hd * HG_VAL_DIM, (hd + 1) * HG_VAL_DIM)
        q = _silu(proj[:, ksl])
        lb = lb_ref[:, ksl]
        f = lb + (1.0 - lb) * _sigmoid(proj[:, HG_KEY_TOTAL:2 * HG_KEY_TOTAL][:, ksl] + bf_ref[:, ksl])
        log_f = jnp.log(f)
        kk = 1.0 - f
        v = proj[:, 2 * HG_KEY_TOTAL:2 * HG_KEY_TOTAL + D_MODEL][:, vsl]
        lf_hi = log_f.astype(BF16)
        lf_lo = (log_f - lf_hi.astype(F32)).astype(BF16)
        b = _dot(tri, lf_hi) + _dot(tri, lf_lo)
        state = state_ref[hd]
        o = _dot_nt(q * jnp.exp(b), state)
        blocks = []
        for sc in range(HG_CHUNK // HG_SUB):
            lo_r, hi_r = sc * HG_SUB, (sc + 1) * HG_SUB
            ref = b[lo_r - 1:lo_r, :] if sc else jnp.zeros((1, HG_KEY_DIM), F32)
            qt = q[lo_r:hi_r] * jnp.exp(b[lo_r:hi_r] - ref)
            kt = kk * jnp.exp(jnp.where(row < hi_r, ref - b, NEG_BIG))
            blocks.append(_dot_nt(qt, kt))
        scores = jnp.where(causal, jnp.concatenate(blocks, axis=0), 0.0)
        o = o + _dot(scores, v)
        b_end = b[HG_CHUNK - 1:HG_CHUNK, :]
        state_ref[hd] = jnp.exp(b_end) * state + _dot_tn(v, kk * jnp.exp(b_end - b))
        outs.append(_rms(o, go_ref[...]))
    gate = _silu(proj[:, 2 * HG_KEY_TOTAL + D_MODEL:])
    y = x + _dot((jnp.concatenate(outs, axis=-1) * gate).astype(BF16), wout_ref[...])
    pos = t * HG_CHUNK + lax.broadcasted_iota(jnp.int32, y.shape, 0)
    o_ref[0] = jnp.where(pos >= pad, y, 0.0)


def _hgrn_layer(h, gain, win, bf, lb, go, wout, pad):
    bsz, lp, _ = h.shape
    hg_in = win.shape[1]
    tri = jnp.tril(jnp.ones((HG_CHUNK, HG_CHUNK), BF16))
    row = pl.BlockSpec((1, HG_CHUNK, D_MODEL), lambda b, t: (b, t, 0))
    return pl.pallas_call(
        functools.partial(_hgrn_kernel, pad=pad),
        grid=(bsz, lp // HG_CHUNK),
        in_specs=[row, _full((1, D_MODEL)), _full((D_MODEL, hg_in)), _full((1, HG_KEY_TOTAL)),
                  _full((1, HG_KEY_TOTAL)), _full((1, HG_VAL_DIM)), _full((D_MODEL, D_MODEL)),
                  _full((HG_CHUNK, HG_CHUNK))],
        out_specs=row,
        out_shape=jax.ShapeDtypeStruct(h.shape, F32),
        scratch_shapes=[pltpu.VMEM((HG_HEADS, HG_VAL_DIM, HG_KEY_DIM), F32)],
        compiler_params=_params(2),
        name="hgrn_layer",
    )(h, gain, win, bf, lb, go, wout, tri)


def _router_kernel(h_ref, gain_ref, w_ref, b_ref, tri_ref, idx_ref, gate_ref, xn_ref, count_ref):
    @pl.when(pl.program_id(0) == 0)
    def _():
        count_ref[...] = jnp.zeros_like(count_ref)

    xn = _rms(h_ref[...], gain_ref[...])
    for s in range(ROW_TILES):
        xn_ref[:, s, :] = xn[:, s * LANES:(s + 1) * LANES]
    logit = _dot(xn, w_ref[...], HIGHEST) + b_ref[...]
    lane = lax.broadcasted_iota(jnp.int32, logit.shape, 1)
    is_grp = (lane >= N_EXPERTS) & (lane < N_EXPERTS + N_GROUPS)
    g_logit = jnp.where(is_grp, logit, -jnp.inf)
    g_max = jnp.max(g_logit, axis=-1, keepdims=True)
    g_idx = jnp.min(jnp.where(g_logit == g_max, lane, 4 * LANES), axis=-1, keepdims=True) - N_EXPERTS
    g_gate = 1.0 / jnp.sum(jnp.exp(g_logit - g_max), axis=-1, keepdims=True)
    in_grp = (lane >= g_idx * EXPERTS_PER_GROUP) & (lane < (g_idx + 1) * EXPERTS_PER_GROUP)
    e1 = jnp.where(in_grp, logit, -jnp.inf)
    v1 = jnp.max(e1, axis=-1, keepdims=True)
    i1 = jnp.min(jnp.where(e1 == v1, lane, 4 * LANES), axis=-1, keepdims=True)
    e2 = jnp.where(lane == i1, -jnp.inf, e1)
    v2 = jnp.max(e2, axis=-1, keepdims=True)
    i2 = jnp.min(jnp.where(e2 == v2, lane, 4 * LANES), axis=-1, keepdims=True)
    z = jnp.exp(v2 - v1)
    w1 = g_gate / (1.0 + z)
    gate_ref[...] = jnp.where(lane == 0, w1, jnp.where(lane == 1, w1 * z, 0.0))
    pick1 = lane == i1
    pick2 = lane == i2
    onehot = jnp.where(pick1 | pick2, 1.0, 0.0)
    before = count_ref[...] + _dot(tri_ref[...], onehot.astype(BF16))
    r1 = jnp.sum(jnp.where(pick1, before, 0.0), axis=-1, keepdims=True).astype(jnp.int32)
    r2 = jnp.sum(jnp.where(pick2, before, 0.0), axis=-1, keepdims=True).astype(jnp.int32)
    tm = onehot.shape[0]
    count_ref[...] = before[tm - 1:tm, :] + onehot[tm - 1:tm, :]
    idx_ref[...] = jnp.where(lane == 0, i1, jnp.where(lane == 1, i2,
                             jnp.where(lane == 2, r1, jnp.where(lane == 3, r2, 0))))


def _router(h2, gain, w_cat, b_cat):
    n = h2.shape[0]
    tm = _row_tile(n)
    row = pl.BlockSpec((tm, D_MODEL), lambda t: (t, 0))
    lanes = pl.BlockSpec((tm, LANES), lambda t: (t, 0))
    tri = jnp.tril(jnp.ones((tm, tm), BF16), -1)
    return pl.pallas_call(
        _router_kernel,
        grid=(n // tm,),
        in_specs=[row, _full((1, D_MODEL)), _full((D_MODEL, LANES)), _full((1, LANES)), _full((tm, tm))],
        out_specs=[lanes, lanes, pl.BlockSpec((tm, ROW_TILES, LANES), lambda t: (t, 0, 0)),
                   _full((1, LANES))],
        out_shape=[jax.ShapeDtypeStruct((n, LANES), jnp.int32),
                   jax.ShapeDtypeStruct((n, LANES), F32),
                   jax.ShapeDtypeStruct((n, ROW_TILES, LANES), F32),
                   jax.ShapeDtypeStruct((1, LANES), F32)],
        compiler_params=_params(1),
        name="moe_router",
    )(h2, gain, w_cat, b_cat, tri)


def _row_copy(src_ref, dst_ref, sem, src_row, dst_row):
    return pltpu.make_async_copy(src_ref.at[pl.ds(src_row, 1)], dst_ref.at[pl.ds(dst_row, 1)], sem)


def _dispatch_kernel(fill_ref, nb_ref, d0_ref, d1_ref, x_ref, o_hbm, zero_ref, sem, *, tm, n_blocks):
    @pl.when(pl.program_id(0) == 0)
    def _():
        zero_ref[...] = jnp.zeros_like(zero_ref)

        def fill_copy(row):
            return pltpu.make_async_copy(zero_ref, o_hbm.at[pl.ds(row, EXPERT_ROWS)], sem.at[2])

        def tail_start(b, carry):
            fill_copy(pl.multiple_of(b * EXPERT_ROWS, EXPERT_ROWS)).start()
            return carry

        def tail_wait(b, carry):
            fill_copy(pl.multiple_of(b * EXPERT_ROWS, EXPERT_ROWS)).wait()
            return carry

        for e in range(N_EXPERTS):
            fill_copy(fill_ref[e]).start()
        for e in range(N_EXPERTS):
            fill_copy(fill_ref[e]).wait()
        lax.fori_loop(nb_ref[0], n_blocks + 1, tail_start, 0)
        lax.fori_loop(nb_ref[0], n_blocks + 1, tail_wait, 0)

    def issue(r, carry):
        _row_copy(x_ref, o_hbm, sem.at[0], r, d0_ref[0, 0, r]).start()
        _row_copy(x_ref, o_hbm, sem.at[1], r, d1_ref[0, 0, r]).start()
        return carry

    lax.fori_loop(0, tm, issue, 0)
    pltpu.make_async_copy(x_ref, o_hbm.at[pl.ds(0, tm)], sem.at[0]).wait()
    pltpu.make_async_copy(x_ref, o_hbm.at[pl.ds(0, tm)], sem.at[1]).wait()


def _dispatch(xn3, dest, fill_start, n_used, n_blocks):
    n = xn3.shape[0]
    tm = _row_tile(n)
    steps = n // tm
    idx = pl.BlockSpec((1, 1, tm), lambda t, fill, nb: (t, 0, 0), memory_space=pltpu.SMEM)
    return pl.pallas_call(
        functools.partial(_dispatch_kernel, tm=tm, n_blocks=n_blocks),
        grid_spec=pltpu.PrefetchScalarGridSpec(
            num_scalar_prefetch=2,
            grid=(steps,),
            in_specs=[idx, idx, pl.BlockSpec((tm, ROW_TILES, LANES), lambda t, fill, nb: (t, 0, 0))],
            out_specs=pl.BlockSpec(memory_space=pl.ANY),
            scratch_shapes=[pltpu.VMEM((EXPERT_ROWS, ROW_TILES, LANES), F32),
                            pltpu.SemaphoreType.DMA((3,))]),
        out_shape=jax.ShapeDtypeStruct(((n_blocks + 1) * EXPERT_ROWS, ROW_TILES, LANES), F32),
        compiler_params=_params(1),
        name="moe_dispatch",
    )(fill_start, n_used, dest[:, 0].reshape(steps, 1, tm), dest[:, 1].reshape(steps, 1, tm), xn3)


def _expert_kernel(be_ref, nb_ref, x_ref, wup_ref, wdn_ref, y_ref):
    del be_ref

    used = pl.program_id(0) < nb_ref[0]

    @pl.when(jnp.logical_not(used))
    def _():
        y_ref[...] = jnp.zeros_like(y_ref)

    @pl.when(used)
    def _():
        x = jnp.concatenate([x_ref[:, s, :] for s in range(ROW_TILES)], axis=-1)
        gu = _dot(x.astype(BF16), wup_ref[0])
        act = _silu(gu[:, :D_EXPERT]) * gu[:, D_EXPERT:]
        y = _dot(act.astype(BF16), wdn_ref[0])
        for s in range(ROW_TILES):
            y_ref[:, s, :] = y[:, s * LANES:(s + 1) * LANES]


def _experts(x_pad3, block_expert, n_used, w_up, w_down, n_blocks):
    blk = pl.BlockSpec((EXPERT_ROWS, ROW_TILES, LANES), lambda i, be, nb: (i, 0, 0))
    return pl.pallas_call(
        _expert_kernel,
        grid_spec=pltpu.PrefetchScalarGridSpec(
            num_scalar_prefetch=2,
            grid=(n_blocks,),
            in_specs=[blk,
                      pl.BlockSpec((1, D_MODEL, 2 * D_EXPERT), lambda i, be, nb: (be[i], 0, 0)),
                      pl.BlockSpec((1, D_EXPERT, D_MODEL), lambda i, be, nb: (be[i], 0, 0))],
            out_specs=blk),
        out_shape=jax.ShapeDtypeStruct((n_blocks * EXPERT_ROWS, ROW_TILES, LANES), F32),
        compiler_params=_params(1),
        name="moe_experts",
    )(block_expert, n_used, x_pad3, w_up, w_down)


def _combine_kernel(d0_ref, d1_ref, y_hbm, h_ref, gate_ref, o_ref, buf0, buf1, sem, *, tm, lp, pad):
    t = pl.program_id(0)

    def issue(r, carry):
        _row_copy(y_hbm, buf0, sem.at[0], d0_ref[0, 0, r], r).start()
        _row_copy(y_hbm, buf1, sem.at[1], d1_ref[0, 0, r], r).start()
        return carry

    lax.fori_loop(0, tm, issue, 0)
    pltpu.make_async_copy(y_hbm.at[pl.ds(0, tm)], buf0, sem.at[0]).wait()
    pltpu.make_async_copy(y_hbm.at[pl.ds(0, tm)], buf1, sem.at[1]).wait()
    g0 = gate_ref[:, 0:1]
    g1 = gate_ref[:, 1:2]
    pos = (t * tm + lax.broadcasted_iota(jnp.int32, (tm, LANES), 0)) % lp
    for s in range(ROW_TILES):
        sl = slice(s * LANES, (s + 1) * LANES)
        y = h_ref[:, sl] + g0 * buf0[:, s, :] + g1 * buf1[:, s, :]
        o_ref[:, sl] = jnp.where(pos >= pad, y, 0.0)


def _combine(y3, dest, gate, h2, lp, pad):
    n = h2.shape[0]
    tm = _row_tile(lp)
    steps = n // tm
    idx = pl.BlockSpec((1, 1, tm), lambda t: (t, 0, 0), memory_space=pltpu.SMEM)
    row = pl.BlockSpec((tm, D_MODEL), lambda t: (t, 0))
    return pl.pallas_call(
        functools.partial(_combine_kernel, tm=tm, lp=lp, pad=pad),
        grid=(steps,),
        in_specs=[idx, idx, pl.BlockSpec(memory_space=pl.ANY), row,
                  pl.BlockSpec((tm, LANES), lambda t: (t, 0))],
        out_specs=row,
        out_shape=jax.ShapeDtypeStruct(h2.shape, F32),
        scratch_shapes=[pltpu.VMEM((tm, ROW_TILES, LANES), F32), pltpu.VMEM((tm, ROW_TILES, LANES), F32),
                        pltpu.SemaphoreType.DMA((2,))],
        compiler_params=_params(1),
        name="moe_combine",
    )(dest[:, 0].reshape(steps, 1, tm), dest[:, 1].reshape(steps, 1, tm), y3, h2, gate)


def _moe_layer(h, gain, w_grp, b_grp, w_rt, b_rt, w_up, w_down, pad):
    bsz, lp, _ = h.shape
    n = bsz * lp
    h2 = h.reshape(n, D_MODEL)
    fill = LANES - N_EXPERTS - N_GROUPS
    w_cat = jnp.concatenate([w_rt, w_grp, jnp.zeros((D_MODEL, fill), F32)], axis=1)
    b_cat = jnp.concatenate([b_rt, b_grp, jnp.zeros((fill,), F32)])[None, :]
    idx, gate, xn3, counts = _router(h2, gain, w_cat, b_cat)

    counts = counts[0, :N_EXPERTS].astype(jnp.int32)
    padded = (counts + EXPERT_ROWS - 1) // EXPERT_ROWS * EXPERT_ROWS
    pend = jnp.cumsum(padded)
    pstart = pend - padded
    dest = pstart[idx[:, :TOP_K]] + idx[:, TOP_K:2 * TOP_K]
    n_blocks = (n * TOP_K + N_EXPERTS * (EXPERT_ROWS - 1) + EXPERT_ROWS - 1) // EXPERT_ROWS
    block_start = jnp.arange(n_blocks, dtype=jnp.int32) * EXPERT_ROWS
    block_expert = jnp.minimum(jnp.sum(block_start[:, None] >= pend[None, :], axis=1),
                               N_EXPERTS - 1).astype(jnp.int32)
    n_used = (pend[N_EXPERTS - 1:] // EXPERT_ROWS).astype(jnp.int32)

    x_pad3 = _dispatch(xn3, dest, pstart + counts, n_used, n_blocks)
    y3 = _experts(x_pad3, block_expert, n_used, w_up, w_down, n_blocks)
    return _combine(y3, dest, gate, h2, lp, pad).reshape(bsz, lp, D_MODEL)


def _lower_bounds(lb_logits):
    p = jax.nn.softmax(lb_logits.astype(F32), axis=0)
    cum = jnp.cumsum(p, axis=0)
    return cum - cum[:1]


@jax.jit
def kernel(x, meta_tokens, fox_norm, fox_w_in, fox_b_f, fox_q_norm, fox_k_norm, fox_w_out, hg_norm, hg_w_in, hg_b_f, hg_lb_logits, hg_o_norm, hg_w_out, moe_norm, moe_w_grp, moe_b_grp, moe_w_rt, moe_b_rt, moe_w_up, moe_w_down):
    bsz, seq, _ = x.shape
    depth = moe_norm.shape[0]
    length = N_META + seq
    pad = (-length) % ROW_ALIGN
    meta = jnp.broadcast_to(meta_tokens[None].astype(x.dtype), (bsz, N_META, D_MODEL))
    h = jnp.concatenate([jnp.zeros((bsz, pad, D_MODEL), x.dtype), meta, x], axis=1)
    lower_bounds = _lower_bounds(hg_lb_logits)
    scale = FOX_HEAD_DIM ** -0.5
    for i in range(depth):
        j = i // 2
        if i % 2 == 0:
            shift = FOX_HEAD_DIM * scale * jnp.max(jnp.abs(fox_q_norm[j])) * jnp.max(jnp.abs(fox_k_norm[j]))
            qa, ka, va = _fox_in(
                h, fox_norm[j][None, :], fox_w_in[j][:, :3 * D_MODEL].astype(BF16),
                fox_w_in[j][:, 3 * D_MODEL:], fox_b_f[j][None, :],
                jnp.tile(fox_q_norm[j] * (scale * LOG2E), FOX_HEADS)[None, :],
                jnp.tile(fox_k_norm[j], FOX_HEADS)[None, :], shift, pad)
            o = _flash(qa, ka, va, h.shape[1])
            h = _out_proj(o, fox_w_out[j].astype(BF16), h, pad)
        else:
            h = _hgrn_layer(h, hg_norm[j][None, :], hg_w_in[j].astype(BF16), hg_b_f[j][None, :],
                            lower_bounds[i][None, :], hg_o_norm[j][None, :],
                            hg_w_out[j].astype(BF16), pad)
        h = _moe_layer(h, moe_norm[i][None, :], moe_w_grp[i], moe_b_grp[i], moe_w_rt[i], moe_b_rt[i],
                       moe_w_up[i].astype(BF16), moe_w_down[i].astype(BF16), pad)
    return h[:, pad + N_META:]
```

```python
import functools

import jax
import jax.numpy as jnp
from jax import lax
from jax.experimental import pallas as pl
from jax.experimental.pallas import tpu as pltpu

F32 = jnp.float32
BF16 = jnp.bfloat16
HIGHEST = lax.Precision.HIGHEST

D_MODEL = 1024
N_META = 16
ROW_ALIGN = 128
EPS = 1e-6
NEG_BIG = -1e30

FOX_HEADS = 16
FOX_HEAD_DIM = D_MODEL // FOX_HEADS
HG_HEADS = 8
HG_KEY_DIM = 128
HG_VAL_DIM = D_MODEL // HG_HEADS
HG_KEY_TOTAL = HG_HEADS * HG_KEY_DIM
HG_CHUNK = 128
HG_SUB = 16

N_GROUPS = 4
EXPERTS_PER_GROUP = 8
N_EXPERTS = N_GROUPS * EXPERTS_PER_GROUP
TOP_K = 2
D_EXPERT = D_MODEL // 2
EXPERT_ROWS = 256

LANES = 128
SUBLANES = 8
VMEM_LIMIT = 56 * 1024 * 1024


def _dot(a, b, precision=None):
    return jnp.dot(a, b, preferred_element_type=F32, precision=precision)


def _dot_nt(a, b):
    return lax.dot_general(a, b, (((1,), (1,)), ((), ())), preferred_element_type=F32)


def _dot_tn(a, b):
    return lax.dot_general(a, b, (((0,), (0,)), ((), ())), preferred_element_type=F32)


def _rms(x, gain):
    return x * lax.rsqrt(jnp.mean(x * x, axis=-1, keepdims=True) + EPS) * gain


def _sigmoid(x):
    return 1.0 / (1.0 + jnp.exp(-x))


def _silu(x):
    return x * _sigmoid(x)


def _row_tile(lp):
    for t in (384, 256, 128):
        if lp % t == 0:
            return t
    raise ValueError(lp)


def _params(n_axes):
    return pltpu.CompilerParams(dimension_semantics=("arbitrary",) * n_axes,
                                vmem_limit_bytes=VMEM_LIMIT)


def _full(shape):
    return pl.BlockSpec(shape, lambda *_: (0,) * len(shape))


ATT_BLOCK = 256
LOG2E = 1.4426950408889634

LANE_C_Q = FOX_HEAD_DIM
LANE_C_K = FOX_HEAD_DIM + 3
LANE_KEY_PAD = FOX_HEAD_DIM + 6
LANE_SHIFT = FOX_HEAD_DIM + 7
LANE_ONE_V = FOX_HEAD_DIM


def _bf16_trunc(x):
    bits = lax.bitcast_convert_type(x, jnp.int32) & jnp.int32(-65536)
    return lax.bitcast_convert_type(bits, F32)


def _split3(c):
    hi = _bf16_trunc(c)
    mid = _bf16_trunc(c - hi)
    return hi, mid, c - hi - mid


def _split2(x):
    hi = x.astype(BF16)
    return hi, (x - hi.astype(F32)).astype(BF16)


def _fox_in_kernel(h_ref, gain_ref, wqkv_ref, wf_ref, bf_ref, gq_ref, gk_ref, seg_ref, segt_ref,
                   tri_ref, rowq_ref, rowk_ref, qa_ref, ka_ref, va_ref, carry_ref, *, tm, nt, pad):
    t = pl.program_id(1)

    @pl.when(t == 0)
    def _():
        carry_ref[...] = jnp.zeros_like(carry_ref)

    @pl.when(t >= nt)
    def _():
        qa_ref[...] = jnp.zeros_like(qa_ref)
        ka_ref[...] = jnp.zeros_like(ka_ref)
        va_ref[...] = jnp.zeros_like(va_ref)

    @pl.when(t < nt)
    def _():
        xn = _rms(h_ref[0], gain_ref[...])
        qkv = _dot(xn.astype(BF16), wqkv_ref[...])

        def head_norm(z, g_ref):
            hi, lo = _split2(z * z)
            ms = (_dot(hi, seg_ref[...]) + _dot(lo, seg_ref[...])) * (1.0 / FOX_HEAD_DIM)
            hi, lo = _split2(lax.rsqrt(ms + EPS))
            return z * (_dot(hi, segt_ref[...]) + _dot(lo, segt_ref[...])) * g_ref[...]

        qn = head_norm(qkv[:, :D_MODEL], gq_ref)
        kn = head_norm(qkv[:, D_MODEL:2 * D_MODEL], gk_ref)
        vv = qkv[:, 2 * D_MODEL:]

        x_hi, x_lo = _split2(xn)
        w_hi, w_lo = _split2(wf_ref[...])
        fl = _dot(x_hi, w_hi) + _dot(x_hi, w_lo) + _dot(x_lo, w_hi) + bf_ref[...]
        log_f = jnp.minimum(fl, 0.0) - jnp.log(1.0 + jnp.exp(-jnp.abs(fl)))
        pos16 = t * tm + lax.broadcasted_iota(jnp.int32, log_f.shape, 0)
        log_f = jnp.where(pos16 >= pad, log_f, 0.0)
        tri = tri_ref[...]
        cum = carry_ref[...] + sum(_dot(tri, piece.astype(BF16)) for piece in _split3(log_f))
        carry_ref[...] = cum[tm - 1:tm, :]
        c_hi, c_mid, c_lo = _split3(cum * LOG2E)

        lane = lax.broadcasted_iota(jnp.int32, (tm, LANES), 1)
        pos = t * tm + lax.broadcasted_iota(jnp.int32, (tm, LANES), 0)
        is_head = lane < FOX_HEAD_DIM
        row_k = jnp.where((lane == LANE_KEY_PAD) & (pos < pad), NEG_BIG, rowk_ref[...])
        row_v = jnp.where(lane == LANE_ONE_V, 1.0, 0.0)
        for hd in range(FOX_HEADS):
            col = slice((hd // 2) * LANES, (hd // 2 + 1) * LANES)

            def head(z):
                z = z[:, col]
                return pltpu.roll(z, FOX_HEAD_DIM, axis=1) if hd % 2 else z

            def spread(piece):
                return jnp.broadcast_to(piece[:, hd:hd + 1], (tm, LANES))

            hi, mid, lo = spread(c_hi), spread(c_mid), spread(c_lo)
            aug_q = jnp.where(lane == LANE_C_Q, hi, jnp.where(lane == LANE_C_Q + 1, mid,
                              jnp.where(lane == LANE_C_Q + 2, lo, rowq_ref[...])))
            aug_k = jnp.where(lane == LANE_C_K, -hi, jnp.where(lane == LANE_C_K + 1, -mid,
                              jnp.where(lane == LANE_C_K + 2, -lo, row_k)))
            qa_ref[0, hd] = jnp.where(is_head, head(qn), aug_q).astype(BF16)
            ka_ref[0, hd] = jnp.where(is_head, head(kn), aug_k).astype(BF16)
            va_ref[0, hd] = jnp.where(is_head, head(vv), row_v).astype(BF16)


def _fox_in(h, gain, wqkv, wf, bf, gq, gk, shift, pad):
    bsz, lp, _ = h.shape
    tm = _row_tile(lp)
    nt = lp // tm
    lpa = -(-lp // ATT_BLOCK) * ATT_BLOCK
    head_of = jnp.arange(D_MODEL) // FOX_HEAD_DIM
    seg = (head_of[:, None] == jnp.arange(FOX_HEADS)[None, :]).astype(BF16)
    tri = jnp.tril(jnp.ones((tm, tm), BF16))
    lane = jnp.arange(LANES)
    row_q = jnp.where((lane >= LANE_C_K) & (lane <= LANE_KEY_PAD), 1.0,
                      jnp.where(lane == LANE_SHIFT, -shift * LOG2E, 0.0)).astype(F32)[None, :]
    row_k = (((lane >= LANE_C_Q) & (lane < LANE_C_K)) | (lane == LANE_SHIFT)).astype(F32)[None, :]
    out = pl.BlockSpec((1, FOX_HEADS, tm, LANES), lambda b, t: (b, 0, t, 0))
    return pl.pallas_call(
        functools.partial(_fox_in_kernel, tm=tm, nt=nt, pad=pad),
        grid=(bsz, -(-lpa // tm)),
        in_specs=[pl.BlockSpec((1, tm, D_MODEL), lambda b, t: (b, jnp.minimum(t, nt - 1), 0)),
                  _full((1, D_MODEL)), _full((D_MODEL, 3 * D_MODEL)),
                  _full((D_MODEL, FOX_HEADS)), _full((1, FOX_HEADS)),
                  _full((1, D_MODEL)), _full((1, D_MODEL)),
                  _full((D_MODEL, FOX_HEADS)), _full((FOX_HEADS, D_MODEL)), _full((tm, tm)),
                  _full((1, LANES)), _full((1, LANES))],
        out_specs=[out, out, out],
        out_shape=[jax.ShapeDtypeStruct((bsz, FOX_HEADS, lpa, LANES), BF16)] * 3,
        scratch_shapes=[pltpu.VMEM((1, FOX_HEADS), F32)],
        compiler_params=_params(2),
        name="fox_in",
    )(h, gain, wqkv, wf, bf, gq, gk, seg, seg.T, tri, row_q, row_k)


ATT_Q_STRIPS = 5


def _flash_kernel(q_ref, k_ref, v_ref, o_ref, acc_ref, *, tq, tk):
    i = pl.program_id(2)
    n_strip = tq // tk
    lane = lax.broadcasted_iota(jnp.int32, (tq, LANES), 1)

    for hd in range(2):
        q = q_ref[0, hd]

        def strip(rows, off, masked):
            s = _dot_nt(q[rows:], k_ref[0, hd, pl.ds(off, tk), :])
            if masked:
                r = lax.broadcasted_iota(jnp.int32, s.shape, 0)
                c = lax.broadcasted_iota(jnp.int32, s.shape, 1)
                s = jnp.where(c <= r, s, NEG_BIG)
            p = jnp.exp2(s).astype(BF16)
            return _dot(p, v_ref[0, hd, pl.ds(off, tk), :])

        def below_diagonal(j, carry):
            total = None
            for n in range(n_strip):
                part = strip(0, pl.multiple_of(j * tq + n * tk, tk), False)
                total = part if total is None else total + part
            acc_ref[...] += total
            return carry

        acc_ref[...] = jnp.zeros_like(acc_ref)
        lax.fori_loop(0, i, below_diagonal, 0)
        for n in range(n_strip):
            acc_ref[n * tk:, :] += strip(n * tk, pl.multiple_of(i * tq + n * tk, tk), True)
        acc = acc_ref[...]
        denom = acc[:, FOX_HEAD_DIM:FOX_HEAD_DIM + 1]
        out = acc / jnp.where(denom > 0.0, denom, 1.0)
        if hd == 0:
            o_ref[0] = out.astype(o_ref.dtype)
        else:
            shifted = pltpu.roll(out, FOX_HEAD_DIM, axis=1).astype(o_ref.dtype)
            o_ref[0] = jnp.where(lane < FOX_HEAD_DIM, o_ref[0], shifted)


def _flash(qa, ka, va, lp):
    bsz, nh, lpa, _ = qa.shape
    tk = ATT_BLOCK
    tq = ATT_Q_STRIPS * tk if lpa % (ATT_Q_STRIPS * tk) == 0 else tk
    qspec = pl.BlockSpec((1, 2, tq, LANES), lambda b, h, i: (b, h, i, 0))
    kvspec = pl.BlockSpec((1, 2, lpa, LANES), lambda b, h, i: (b, h, 0, 0))
    return pl.pallas_call(
        functools.partial(_flash_kernel, tq=tq, tk=tk),
        grid=(bsz, nh // 2, lpa // tq),
        in_specs=[qspec, kvspec, kvspec],
        out_specs=pl.BlockSpec((1, tq, LANES), lambda b, h, i: (b, i, h)),
        out_shape=jax.ShapeDtypeStruct((bsz, lp, D_MODEL), BF16),
        scratch_shapes=[pltpu.VMEM((tq, LANES), F32)],
        compiler_params=_params(3),
        name="fox_flash",
    )(qa, ka, va)


def _out_proj_kernel(a_ref, w_ref, h_ref, o_ref, *, tm, pad):
    t = pl.program_id(1)
    y = h_ref[0] + _dot(a_ref[0], w_ref[...])
    pos = t * tm + lax.broadcasted_iota(jnp.int32, y.shape, 0)
    o_ref[0] = jnp.where(pos >= pad, y, 0.0)


def _out_proj(a, w, h, pad):
    bsz, lp, _ = h.shape
    tm = _row_tile(lp)
    row = pl.BlockSpec((1, tm, D_MODEL), lambda b, t: (b, t, 0))
    return pl.pallas_call(
        functools.partial(_out_proj_kernel, tm=tm, pad=pad),
        grid=(bsz, lp // tm),
        in_specs=[row, _full((D_MODEL, D_MODEL)), row],
        out_specs=row,
        out_shape=jax.ShapeDtypeStruct(h.shape, F32),
        compiler_params=_params(2),
        name="out_proj",
    )(a, w, h)


def _hgrn_kernel(h_ref, gain_ref, win_ref, bf_ref, lb_ref, go_ref, wout_ref, tri_ref, o_ref,
                 state_ref, *, pad, n_sub):
    t = pl.program_id(1)

    @pl.when(t == 0)
    def _():
        state_ref[...] = jnp.zeros_like(state_ref)

    x = h_ref[0]
    proj = _dot(_rms(x, gain_ref[...]).astype(BF16), win_ref[...])
    tri = tri_ref[...]
    row = lax.broadcasted_iota(jnp.int32, (HG_CHUNK, HG_KEY_DIM), 0)
    causal = (lax.broadcasted_iota(jnp.int32, (HG_CHUNK, HG_CHUNK), 1)
              <= lax.broadcasted_iota(jnp.int32, (HG_CHUNK, HG_CHUNK), 0))
    outs = [[] for _ in range(n_sub)]
    for hd in range(HG_HEADS):
        ksl = slice(hd * HG_KEY_DIM, (hd + 1) * HG_KEY_DIM)
        vsl = slice(hd * HG_VAL_DIM, (hd + 1) * HG_VAL_DIM)
        q_all = _silu(proj[:, ksl])
        lb = lb_ref[:, ksl]
        f = lb + (1.0 - lb) * _sigmoid(proj[:, HG_KEY_TOTAL:2 * HG_KEY_TOTAL][:, ksl] + bf_ref[:, ksl])
        log_f = jnp.log(f)
        kk_all = 1.0 - f
        v_all = proj[:, 2 * HG_KEY_TOTAL:2 * HG_KEY_TOTAL + D_MODEL][:, vsl]
        lf_hi_all, lf_lo_all = _split2(log_f)
        state = state_ref[hd]
        for c in range(n_sub):
            rs = slice(c * HG_CHUNK, (c + 1) * HG_CHUNK)
            q, kk, v = q_all[rs], kk_all[rs], v_all[rs]
            b = _dot(tri, lf_hi_all[rs]) + _dot(tri, lf_lo_all[rs])
            o = _dot_nt(q * jnp.exp(b), state)
            blocks = []
            for sc in range(HG_CHUNK // HG_SUB):
                lo_r, hi_r = sc * HG_SUB, (sc + 1) * HG_SUB
                ref = b[lo_r - 1:lo_r, :] if sc else jnp.zeros((1, HG_KEY_DIM), F32)
                qt = q[lo_r:hi_r] * jnp.exp(b[lo_r:hi_r] - ref)
                kt = kk * jnp.exp(jnp.where(row < hi_r, ref - b, NEG_BIG))
                blocks.append(_dot_nt(qt, kt))
            scores = jnp.where(causal, jnp.concatenate(blocks, axis=0), 0.0)
            o = o + _dot(scores, v)
            b_end = b[HG_CHUNK - 1:HG_CHUNK, :]
            state = jnp.exp(b_end) * state + _dot_tn(v, kk * jnp.exp(b_end - b))
            outs[c].append(_rms(o, go_ref[...]))
        state_ref[hd] = state
    gate = _silu(proj[:, 2 * HG_KEY_TOTAL + D_MODEL:])
    o_all = jnp.concatenate([jnp.concatenate(heads, axis=-1) for heads in outs], axis=0)
    y = x + _dot((o_all * gate).astype(BF16), wout_ref[...])
    pos = t * (n_sub * HG_CHUNK) + lax.broadcasted_iota(jnp.int32, y.shape, 0)
    o_ref[0] = jnp.where(pos >= pad, y, 0.0)


def _hgrn_layer(h, gain, win, bf, lb, go, wout, pad):
    bsz, lp, _ = h.shape
    hg_in = win.shape[1]
    n_sub = _row_tile(lp) // HG_CHUNK
    tri = jnp.tril(jnp.ones((HG_CHUNK, HG_CHUNK), BF16))
    row = pl.BlockSpec((1, n_sub * HG_CHUNK, D_MODEL), lambda b, t: (b, t, 0))
    return pl.pallas_call(
        functools.partial(_hgrn_kernel, pad=pad, n_sub=n_sub),
        grid=(bsz, lp // (n_sub * HG_CHUNK)),
        in_specs=[row, _full((1, D_MODEL)), _full((D_MODEL, hg_in)), _full((1, HG_KEY_TOTAL)),
                  _full((1, HG_KEY_TOTAL)), _full((1, HG_VAL_DIM)), _full((D_MODEL, D_MODEL)),
                  _full((HG_CHUNK, HG_CHUNK))],
        out_specs=row,
        out_shape=jax.ShapeDtypeStruct(h.shape, F32),
        scratch_shapes=[pltpu.VMEM((HG_HEADS, HG_VAL_DIM, HG_KEY_DIM), F32)],
        compiler_params=_params(2),
        name="hgrn_layer",
    )(h, gain, win, bf, lb, go, wout, tri)


def _router_kernel(h_ref, gain_ref, w_ref, b_ref, tri_ref, idx_ref, gate_ref, xn_ref, count_ref):
    @pl.when(pl.program_id(0) == 0)
    def _():
        count_ref[...] = jnp.zeros_like(count_ref)

    xn = _rms(h_ref[...], gain_ref[...])
    xn_ref[...] = xn
    x_hi, x_lo = _split2(xn)
    w_hi, w_lo = _split2(w_ref[...])
    logit = _dot(x_hi, w_hi) + _dot(x_hi, w_lo) + _dot(x_lo, w_hi) + b_ref[...]
    lane = lax.broadcasted_iota(jnp.int32, logit.shape, 1)
    is_grp = (lane >= N_EXPERTS) & (lane < N_EXPERTS + N_GROUPS)
    g_logit = jnp.where(is_grp, logit, -jnp.inf)
    g_max = jnp.max(g_logit, axis=-1, keepdims=True)
    g_idx = jnp.min(jnp.where(g_logit == g_max, lane, 4 * LANES), axis=-1, keepdims=True) - N_EXPERTS
    g_gate = 1.0 / jnp.sum(jnp.exp(g_logit - g_max), axis=-1, keepdims=True)
    in_grp = (lane >= g_idx * EXPERTS_PER_GROUP) & (lane < (g_idx + 1) * EXPERTS_PER_GROUP)
    e1 = jnp.where(in_grp, logit, -jnp.inf)
    v1 = jnp.max(e1, axis=-1, keepdims=True)
    i1 = jnp.min(jnp.where(e1 == v1, lane, 4 * LANES), axis=-1, keepdims=True)
    e2 = jnp.where(lane == i1, -jnp.inf, e1)
    v2 = jnp.max(e2, axis=-1, keepdims=True)
    i2 = jnp.min(jnp.where(e2 == v2, lane, 4 * LANES), axis=-1, keepdims=True)
    z = jnp.exp(v2 - v1)
    w1 = g_gate / (1.0 + z)
    gate_ref[...] = jnp.where(lane == 0, w1, jnp.where(lane == 1, w1 * z, 0.0))
    pick1 = lane == i1
    pick2 = lane == i2
    onehot = jnp.where(pick1 | pick2, 1.0, 0.0)
    before = count_ref[...] + _dot(tri_ref[...], onehot.astype(BF16))
    r1 = jnp.sum(jnp.where(pick1, before, 0.0), axis=-1, keepdims=True).astype(jnp.int32)
    r2 = jnp.sum(jnp.where(pick2, before, 0.0), axis=-1, keepdims=True).astype(jnp.int32)
    tm = onehot.shape[0]
    count_ref[...] = before[tm - 1:tm, :] + onehot[tm - 1:tm, :]
    idx_ref[...] = jnp.where(lane == 0, i1, jnp.where(lane == 1, i2,
                             jnp.where(lane == 2, r1, jnp.where(lane == 3, r2, 0))))


def _router(h2, gain, w_cat, b_cat):
    n = h2.shape[0]
    tm = _row_tile(n)
    row = pl.BlockSpec((tm, D_MODEL), lambda t: (t, 0))
    lanes = pl.BlockSpec((tm, LANES), lambda t: (t, 0))
    tri = jnp.tril(jnp.ones((tm, tm), BF16), -1)
    return pl.pallas_call(
        _router_kernel,
        grid=(n // tm,),
        in_specs=[row, _full((1, D_MODEL)), _full((D_MODEL, LANES)), _full((1, LANES)), _full((tm, tm))],
        out_specs=[lanes, lanes, row,
                   _full((1, LANES))],
        out_shape=[jax.ShapeDtypeStruct((n, LANES), jnp.int32),
                   jax.ShapeDtypeStruct((n, LANES), F32),
                   jax.ShapeDtypeStruct((n, D_MODEL), F32),
                   jax.ShapeDtypeStruct((1, LANES), F32)],
        compiler_params=_params(1),
        name="moe_router",
    )(h2, gain, w_cat, b_cat, tri)


def _row_copy(src_ref, dst_ref, sem, src_row, dst_row):
    return pltpu.make_async_copy(src_ref.at[pl.ds(src_row, 1)], dst_ref.at[pl.ds(dst_row, 1)], sem)


def _dispatch_kernel(fill_ref, nb_ref, d0_ref, d1_ref, x_ref, o_hbm, zero_ref, sem, *, tm, n_blocks):
    @pl.when(pl.program_id(0) == 0)
    def _():
        zero_ref[...] = jnp.zeros_like(zero_ref)

        def fill_copy(row):
            return pltpu.make_async_copy(zero_ref, o_hbm.at[pl.ds(row, EXPERT_ROWS)], sem.at[2])

        def tail_start(b, carry):
            fill_copy(pl.multiple_of(b * EXPERT_ROWS, EXPERT_ROWS)).start()
            return carry

        def tail_wait(b, carry):
            fill_copy(pl.multiple_of(b * EXPERT_ROWS, EXPERT_ROWS)).wait()
            return carry

        for e in range(N_EXPERTS):
            fill_copy(pl.multiple_of(fill_ref[e], SUBLANES)).start()
        for e in range(N_EXPERTS):
            fill_copy(pl.multiple_of(fill_ref[e], SUBLANES)).wait()
        lax.fori_loop(nb_ref[0], n_blocks + 1, tail_start, 0)
        lax.fori_loop(nb_ref[0], n_blocks + 1, tail_wait, 0)

    def issue(r, carry):
        _row_copy(x_ref, o_hbm, sem.at[0], r, d0_ref[0, 0, r]).start()
        _row_copy(x_ref, o_hbm, sem.at[1], r, d1_ref[0, 0, r]).start()
        return carry

    lax.fori_loop(0, tm, issue, 0)
    pltpu.make_async_copy(x_ref, o_hbm.at[pl.ds(0, tm)], sem.at[0]).wait()
    pltpu.make_async_copy(x_ref, o_hbm.at[pl.ds(0, tm)], sem.at[1]).wait()


def _dispatch(xn3, dest, fill_start, n_used, n_blocks):
    n = xn3.shape[0]
    tm = 2 * _row_tile(n) if n % (2 * _row_tile(n)) == 0 else _row_tile(n)
    steps = n // tm
    idx = pl.BlockSpec((1, 1, tm), lambda t, fill, nb: (t, 0, 0), memory_space=pltpu.SMEM)
    return pl.pallas_call(
        functools.partial(_dispatch_kernel, tm=tm, n_blocks=n_blocks),
        grid_spec=pltpu.PrefetchScalarGridSpec(
            num_scalar_prefetch=2,
            grid=(steps,),
            in_specs=[idx, idx, pl.BlockSpec((tm, D_MODEL), lambda t, fill, nb: (t, 0))],
            out_specs=pl.BlockSpec(memory_space=pl.ANY),
            scratch_shapes=[pltpu.VMEM((EXPERT_ROWS, D_MODEL), F32),
                            pltpu.SemaphoreType.DMA((3,))]),
        out_shape=jax.ShapeDtypeStruct(((n_blocks + 1) * EXPERT_ROWS, D_MODEL), F32),
        compiler_params=_params(1),
        name="moe_dispatch",
    )(fill_start, n_used, dest[:, 0].reshape(steps, 1, tm), dest[:, 1].reshape(steps, 1, tm), xn3)


def _expert_kernel(be_ref, nb_ref, x_ref, wup_ref, wdn_ref, y_ref):
    del be_ref

    used = pl.program_id(0) < nb_ref[0]

    @pl.when(jnp.logical_not(used))
    def _():
        y_ref[...] = jnp.zeros_like(y_ref)

    @pl.when(used)
    def _():
        gu = _dot(x_ref[...].astype(BF16), wup_ref[0])
        act = _silu(gu[:, :D_EXPERT]) * gu[:, D_EXPERT:]
        y_ref[...] = _dot(act.astype(BF16), wdn_ref[0])


def _experts(x_pad3, block_expert, n_used, w_up, w_down, n_blocks):
    blk = pl.BlockSpec((EXPERT_ROWS, D_MODEL), lambda i, be, nb: (i, 0))
    return pl.pallas_call(
        _expert_kernel,
        grid_spec=pltpu.PrefetchScalarGridSpec(
            num_scalar_prefetch=2,
            grid=(n_blocks,),
            in_specs=[blk,
                      pl.BlockSpec((1, D_MODEL, 2 * D_EXPERT), lambda i, be, nb: (be[i], 0, 0)),
                      pl.BlockSpec((1, D_EXPERT, D_MODEL), lambda i, be, nb: (be[i], 0, 0))],
            out_specs=blk),
        out_shape=jax.ShapeDtypeStruct((n_blocks * EXPERT_ROWS, D_MODEL), F32),
        compiler_params=_params(1),
        name="moe_experts",
    )(block_expert, n_used, x_pad3, w_up, w_down)


def _combine_kernel(d0_ref, d1_ref, d0n_ref, d1n_ref, y_hbm, h_ref, gate_ref, o_ref, buf, sem,
                    *, tm, steps, lp, pad):
    t = pl.program_id(0)
    slot = t % 2

    def fetch(i0_ref, i1_ref, into):
        def issue(r, carry):
            _row_copy(y_hbm, buf.at[2 * into], sem.at[2 * into], i0_ref[0, 0, r], r).start()
            _row_copy(y_hbm, buf.at[2 * into + 1], sem.at[2 * into + 1], i1_ref[0, 0, r], r).start()
            return carry

        lax.fori_loop(0, tm, issue, 0)

    @pl.when(t == 0)
    def _():
        fetch(d0_ref, d1_ref, slot)

    @pl.when(t + 1 < steps)
    def _():
        fetch(d0n_ref, d1n_ref, 1 - slot)

    for k in range(TOP_K):
        pltpu.make_async_copy(y_hbm.at[pl.ds(0, tm)], buf.at[2 * slot + k], sem.at[2 * slot + k]).wait()
    g0 = gate_ref[:, 0:1]
    g1 = gate_ref[:, 1:2]
    pos = (t * tm + lax.broadcasted_iota(jnp.int32, (tm, D_MODEL), 0)) % lp
    y = h_ref[...] + g0 * buf[2 * slot] + g1 * buf[2 * slot + 1]
    o_ref[...] = jnp.where(pos >= pad, y, 0.0)


def _combine(y3, dest, gate, h2, lp, pad):
    n = h2.shape[0]
    tm = _row_tile(lp)
    steps = n // tm
    idx = pl.BlockSpec((1, 1, tm), lambda t: (t, 0, 0), memory_space=pltpu.SMEM)
    idx_next = pl.BlockSpec((1, 1, tm), lambda t: (jnp.minimum(t + 1, steps - 1), 0, 0),
                            memory_space=pltpu.SMEM)
    row = pl.BlockSpec((tm, D_MODEL), lambda t: (t, 0))
    d0 = dest[:, 0].reshape(steps, 1, tm)
    d1 = dest[:, 1].reshape(steps, 1, tm)
    return pl.pallas_call(
        functools.partial(_combine_kernel, tm=tm, steps=steps, lp=lp, pad=pad),
        grid=(steps,),
        in_specs=[idx, idx, idx_next, idx_next, pl.BlockSpec(memory_space=pl.ANY), row,
                  pl.BlockSpec((tm, LANES), lambda t: (t, 0))],
        out_specs=row,
        out_shape=jax.ShapeDtypeStruct(h2.shape, F32),
        scratch_shapes=[pltpu.VMEM((2 * TOP_K, tm, D_MODEL), F32),
                        pltpu.SemaphoreType.DMA((2 * TOP_K,))],
        compiler_params=_params(1),
        name="moe_combine",
    )(d0, d1, d0, d1, y3, h2, gate)


def _moe_layer(h, gain, w_grp, b_grp, w_rt, b_rt, w_up, w_down, pad):
    bsz, lp, _ = h.shape
    n = bsz * lp
    h2 = h.reshape(n, D_MODEL)
    fill = LANES - N_EXPERTS - N_GROUPS
    w_cat = jnp.concatenate([w_rt, w_grp, jnp.zeros((D_MODEL, fill), F32)], axis=1)
    b_cat = jnp.concatenate([b_rt, b_grp, jnp.zeros((fill,), F32)])[None, :]
    idx, gate, xn3, counts = _router(h2, gain, w_cat, b_cat)

    counts = counts[0, :N_EXPERTS].astype(jnp.int32)
    padded = (counts + EXPERT_ROWS - 1) // EXPERT_ROWS * EXPERT_ROWS
    pend = jnp.cumsum(padded)
    pstart = pend - padded
    dest = pstart[idx[:, :TOP_K]] + idx[:, TOP_K:2 * TOP_K]
    n_blocks = (n * TOP_K + N_EXPERTS * (EXPERT_ROWS - 1) + EXPERT_ROWS - 1) // EXPERT_ROWS
    block_start = jnp.arange(n_blocks, dtype=jnp.int32) * EXPERT_ROWS
    block_expert = jnp.minimum(jnp.sum(block_start[:, None] >= pend[None, :], axis=1),
                               N_EXPERTS - 1).astype(jnp.int32)
    n_used = (pend[N_EXPERTS - 1:] // EXPERT_ROWS).astype(jnp.int32)

    fill_start = (pstart + counts) // SUBLANES * SUBLANES
    x_pad3 = _dispatch(xn3, dest, fill_start, n_used, n_blocks)
    y3 = _experts(x_pad3, block_expert, n_used, w_up, w_down, n_blocks)
    return _combine(y3, dest, gate, h2, lp, pad).reshape(bsz, lp, D_MODEL)


def _lower_bounds(lb_logits):
    p = jax.nn.softmax(lb_logits.astype(F32), axis=0)
    cum = jnp.cumsum(p, axis=0)
    return cum - cum[:1]


@jax.jit
def kernel(x, meta_tokens, fox_norm, fox_w_in, fox_b_f, fox_q_norm, fox_k_norm, fox_w_out, hg_norm, hg_w_in, hg_b_f, hg_lb_logits, hg_o_norm, hg_w_out, moe_norm, moe_w_grp, moe_b_grp, moe_w_rt, moe_b_rt, moe_w_up, moe_w_down):
    bsz, seq, _ = x.shape
    depth = moe_norm.shape[0]
    length = N_META + seq
    pad = (-length) % ROW_ALIGN
    meta = jnp.broadcast_to(meta_tokens[None].astype(x.dtype), (bsz, N_META, D_MODEL))
    h = jnp.concatenate([jnp.zeros((bsz, pad, D_MODEL), x.dtype), meta, x], axis=1)
    lower_bounds = _lower_bounds(hg_lb_logits)
    scale = FOX_HEAD_DIM ** -0.5
    for i in range(depth):
        j = i // 2
        if i % 2 == 0:
            shift = FOX_HEAD_DIM * scale * jnp.max(jnp.abs(fox_q_norm[j])) * jnp.max(jnp.abs(fox_k_norm[j]))
            qa, ka, va = _fox_in(
                h, fox_norm[j][None, :], fox_w_in[j][:, :3 * D_MODEL].astype(BF16),
                fox_w_in[j][:, 3 * D_MODEL:], fox_b_f[j][None, :],
                jnp.tile(fox_q_norm[j] * (scale * LOG2E), FOX_HEADS)[None, :],
                jnp.tile(fox_k_norm[j], FOX_HEADS)[None, :], shift, pad)
            o = _flash(qa, ka, va, h.shape[1])
            h = _out_proj(o, fox_w_out[j].astype(BF16), h, pad)
        else:
            h = _hgrn_layer(h, hg_norm[j][None, :], hg_w_in[j].astype(BF16), hg_b_f[j][None, :],
                            lower_bounds[i][None, :], hg_o_norm[j][None, :],
                            hg_w_out[j].astype(BF16), pad)
        h = _moe_layer(h, moe_norm[i][None, :], moe_w_grp[i], moe_b_grp[i], moe_w_rt[i], moe_b_rt[i],
                       moe_w_up[i].astype(BF16), moe_w_down[i].astype(BF16), pad)
    return h[:, pad + N_META:]
```

```python
import functools

import jax
import jax.numpy as jnp
from jax import lax
from jax.experimental import pallas as pl
from jax.experimental.pallas import tpu as pltpu

F32 = jnp.float32
BF16 = jnp.bfloat16
HIGHEST = lax.Precision.HIGHEST

D_MODEL = 1024
N_META = 16
ROW_ALIGN = 128
EPS = 1e-6
NEG_BIG = -1e30

FOX_HEADS = 16
FOX_HEAD_DIM = D_MODEL // FOX_HEADS
HG_HEADS = 8
HG_KEY_DIM = 128
HG_VAL_DIM = D_MODEL // HG_HEADS
HG_KEY_TOTAL = HG_HEADS * HG_KEY_DIM
HG_CHUNK = 128
HG_SUB = 16

N_GROUPS = 4
EXPERTS_PER_GROUP = 8
N_EXPERTS = N_GROUPS * EXPERTS_PER_GROUP
TOP_K = 2
D_EXPERT = D_MODEL // 2
EXPERT_ROWS = 256

LANES = 128
SUBLANES = 8
VMEM_LIMIT = 56 * 1024 * 1024


def _dot(a, b, precision=None):
    return jnp.dot(a, b, preferred_element_type=F32, precision=precision)


def _dot_nt(a, b):
    return lax.dot_general(a, b, (((1,), (1,)), ((), ())), preferred_element_type=F32)


def _dot_tn(a, b):
    return lax.dot_general(a, b, (((0,), (0,)), ((), ())), preferred_element_type=F32)


def _rms(x, gain):
    return x * lax.rsqrt(jnp.mean(x * x, axis=-1, keepdims=True) + EPS) * gain


def _sigmoid(x):
    return 1.0 / (1.0 + jnp.exp(-x))


def _silu(x):
    return x * _sigmoid(x)


def _row_tile(lp):
    for t in (384, 256, 128):
        if lp % t == 0:
            return t
    raise ValueError(lp)


def _params(n_axes):
    return pltpu.CompilerParams(dimension_semantics=("arbitrary",) * n_axes,
                                vmem_limit_bytes=VMEM_LIMIT)


def _full(shape):
    return pl.BlockSpec(shape, lambda *_: (0,) * len(shape))


ATT_BLOCK = 256
LOG2E = 1.4426950408889634

LANE_C_Q = FOX_HEAD_DIM
LANE_C_K = FOX_HEAD_DIM + 3
LANE_KEY_PAD = FOX_HEAD_DIM + 6
LANE_SHIFT = FOX_HEAD_DIM + 7
LANE_ONE_V = FOX_HEAD_DIM


def _bf16_trunc(x):
    bits = lax.bitcast_convert_type(x, jnp.int32) & jnp.int32(-65536)
    return lax.bitcast_convert_type(bits, F32)


def _split3(c):
    hi = _bf16_trunc(c)
    mid = _bf16_trunc(c - hi)
    return hi, mid, c - hi - mid


def _split2(x):
    hi = x.astype(BF16)
    return hi, (x - hi.astype(F32)).astype(BF16)


def _fox_in_kernel(h_ref, gain_ref, wqkv_ref, wf_ref, bf_ref, gq_ref, gk_ref, seg_ref, segt_ref,
                   tri_ref, rowq_ref, rowk_ref, qa_ref, ka_ref, va_ref, carry_ref, *, tm, nt, pad):
    t = pl.program_id(1)

    @pl.when(t == 0)
    def _():
        carry_ref[...] = jnp.zeros_like(carry_ref)

    @pl.when(t >= nt)
    def _():
        qa_ref[...] = jnp.zeros_like(qa_ref)
        ka_ref[...] = jnp.zeros_like(ka_ref)
        va_ref[...] = jnp.zeros_like(va_ref)

    @pl.when(t < nt)
    def _():
        xn = _rms(h_ref[0], gain_ref[...])
        qkv = _dot(xn.astype(BF16), wqkv_ref[...])

        def head_norm(z, g_ref):
            hi, lo = _split2(z * z)
            ms = (_dot(hi, seg_ref[...]) + _dot(lo, seg_ref[...])) * (1.0 / FOX_HEAD_DIM)
            hi, lo = _split2(lax.rsqrt(ms + EPS))
            return z * (_dot(hi, segt_ref[...]) + _dot(lo, segt_ref[...])) * g_ref[...]

        qn = head_norm(qkv[:, :D_MODEL], gq_ref)
        kn = head_norm(qkv[:, D_MODEL:2 * D_MODEL], gk_ref)
        vv = qkv[:, 2 * D_MODEL:]

        x_hi, x_lo = _split2(xn)
        w_hi, w_lo = _split2(wf_ref[...])
        fl = _dot(x_hi, w_hi) + _dot(x_hi, w_lo) + _dot(x_lo, w_hi) + bf_ref[...]
        log_f = jnp.minimum(fl, 0.0) - jnp.log(1.0 + jnp.exp(-jnp.abs(fl)))
        pos16 = t * tm + lax.broadcasted_iota(jnp.int32, log_f.shape, 0)
        log_f = jnp.where(pos16 >= pad, log_f, 0.0)
        tri = tri_ref[...]
        cum = carry_ref[...] + sum(_dot(tri, piece.astype(BF16)) for piece in _split3(log_f))
        carry_ref[...] = cum[tm - 1:tm, :]
        c_hi, c_mid, c_lo = _split3(cum * LOG2E)

        lane = lax.broadcasted_iota(jnp.int32, (tm, LANES), 1)
        pos = t * tm + lax.broadcasted_iota(jnp.int32, (tm, LANES), 0)
        is_head = lane < FOX_HEAD_DIM
        row_k = jnp.where((lane == LANE_KEY_PAD) & (pos < pad), NEG_BIG, rowk_ref[...])
        row_v = jnp.where(lane == LANE_ONE_V, 1.0, 0.0)
        for hd in range(FOX_HEADS):
            col = slice((hd // 2) * LANES, (hd // 2 + 1) * LANES)

            def head(z):
                z = z[:, col]
                return pltpu.roll(z, FOX_HEAD_DIM, axis=1) if hd % 2 else z

            def spread(piece):
                return jnp.broadcast_to(piece[:, hd:hd + 1], (tm, LANES))

            hi, mid, lo = spread(c_hi), spread(c_mid), spread(c_lo)
            aug_q = jnp.where(lane == LANE_C_Q, hi, jnp.where(lane == LANE_C_Q + 1, mid,
                              jnp.where(lane == LANE_C_Q + 2, lo, rowq_ref[...])))
            aug_k = jnp.where(lane == LANE_C_K, -hi, jnp.where(lane == LANE_C_K + 1, -mid,
                              jnp.where(lane == LANE_C_K + 2, -lo, row_k)))
            qa_ref[0, hd] = jnp.where(is_head, head(qn), aug_q).astype(BF16)
            ka_ref[0, hd] = jnp.where(is_head, head(kn), aug_k).astype(BF16)
            va_ref[0, hd] = jnp.where(is_head, head(vv), row_v).astype(BF16)


def _fox_in(h, gain, wqkv, wf, bf, gq, gk, shift, pad):
    bsz, lp, _ = h.shape
    tm = _row_tile(lp)
    nt = lp // tm
    lpa = -(-lp // ATT_BLOCK) * ATT_BLOCK
    head_of = jnp.arange(D_MODEL) // FOX_HEAD_DIM
    seg = (head_of[:, None] == jnp.arange(FOX_HEADS)[None, :]).astype(BF16)
    tri = jnp.tril(jnp.ones((tm, tm), BF16))
    lane = jnp.arange(LANES)
    row_q = jnp.where((lane >= LANE_C_K) & (lane <= LANE_KEY_PAD), 1.0,
                      jnp.where(lane == LANE_SHIFT, -shift * LOG2E, 0.0)).astype(F32)[None, :]
    row_k = (((lane >= LANE_C_Q) & (lane < LANE_C_K)) | (lane == LANE_SHIFT)).astype(F32)[None, :]
    out = pl.BlockSpec((1, FOX_HEADS, tm, LANES), lambda b, t: (b, 0, t, 0))
    return pl.pallas_call(
        functools.partial(_fox_in_kernel, tm=tm, nt=nt, pad=pad),
        grid=(bsz, -(-lpa // tm)),
        in_specs=[pl.BlockSpec((1, tm, D_MODEL), lambda b, t: (b, jnp.minimum(t, nt - 1), 0)),
                  _full((1, D_MODEL)), _full((D_MODEL, 3 * D_MODEL)),
                  _full((D_MODEL, FOX_HEADS)), _full((1, FOX_HEADS)),
                  _full((1, D_MODEL)), _full((1, D_MODEL)),
                  _full((D_MODEL, FOX_HEADS)), _full((FOX_HEADS, D_MODEL)), _full((tm, tm)),
                  _full((1, LANES)), _full((1, LANES))],
        out_specs=[out, out, out],
        out_shape=[jax.ShapeDtypeStruct((bsz, FOX_HEADS, lpa, LANES), BF16)] * 3,
        scratch_shapes=[pltpu.VMEM((1, FOX_HEADS), F32)],
        compiler_params=_params(2),
        name="fox_in",
    )(h, gain, wqkv, wf, bf, gq, gk, seg, seg.T, tri, row_q, row_k)


ATT_Q_STRIPS = 5


def _flash_kernel(q_ref, k_ref, v_ref, o_ref, acc_ref, *, tq, tk):
    i = pl.program_id(2)
    n_strip = tq // tk
    lane = lax.broadcasted_iota(jnp.int32, (tq, LANES), 1)

    for hd in range(2):
        q = q_ref[0, hd]

        def strip(rows, off, masked):
            s = _dot_nt(q[rows:], k_ref[0, hd, pl.ds(off, tk), :])
            if masked:
                r = lax.broadcasted_iota(jnp.int32, s.shape, 0)
                c = lax.broadcasted_iota(jnp.int32, s.shape, 1)
                s = jnp.where(c <= r, s, NEG_BIG)
            p = jnp.exp2(s).astype(BF16)
            return _dot(p, v_ref[0, hd, pl.ds(off, tk), :])

        def below_diagonal(j, carry):
            total = None
            for n in range(n_strip):
                part = strip(0, pl.multiple_of(j * tq + n * tk, tk), False)
                total = part if total is None else total + part
            acc_ref[...] += total
            return carry

        acc_ref[...] = jnp.zeros_like(acc_ref)
        lax.fori_loop(0, i, below_diagonal, 0)
        for n in range(n_strip):
            acc_ref[n * tk:, :] += strip(n * tk, pl.multiple_of(i * tq + n * tk, tk), True)
        acc = acc_ref[...]
        denom = acc[:, FOX_HEAD_DIM:FOX_HEAD_DIM + 1]
        out = acc / jnp.where(denom > 0.0, denom, 1.0)
        if hd == 0:
            o_ref[0] = out.astype(o_ref.dtype)
        else:
            shifted = pltpu.roll(out, FOX_HEAD_DIM, axis=1).astype(o_ref.dtype)
            o_ref[0] = jnp.where(lane < FOX_HEAD_DIM, o_ref[0], shifted)


def _flash(qa, ka, va, lp):
    bsz, nh, lpa, _ = qa.shape
    tk = ATT_BLOCK
    tq = ATT_Q_STRIPS * tk if lpa % (ATT_Q_STRIPS * tk) == 0 else tk
    qspec = pl.BlockSpec((1, 2, tq, LANES), lambda b, h, i: (b, h, i, 0))
    kvspec = pl.BlockSpec((1, 2, lpa, LANES), lambda b, h, i: (b, h, 0, 0))
    return pl.pallas_call(
        functools.partial(_flash_kernel, tq=tq, tk=tk),
        grid=(bsz, nh // 2, lpa // tq),
        in_specs=[qspec, kvspec, kvspec],
        out_specs=pl.BlockSpec((1, tq, LANES), lambda b, h, i: (b, i, h)),
        out_shape=jax.ShapeDtypeStruct((bsz, lp, D_MODEL), BF16),
        scratch_shapes=[pltpu.VMEM((tq, LANES), F32)],
        compiler_params=_params(3),
        name="fox_flash",
    )(qa, ka, va)


def _out_proj_kernel(a_ref, w_ref, h_ref, o_ref, *, tm, pad):
    t = pl.program_id(1)
    y = h_ref[0] + _dot(a_ref[0], w_ref[...])
    pos = t * tm + lax.broadcasted_iota(jnp.int32, y.shape, 0)
    o_ref[0] = jnp.where(pos >= pad, y, 0.0)


def _out_proj(a, w, h, pad):
    bsz, lp, _ = h.shape
    tm = _row_tile(lp)
    row = pl.BlockSpec((1, tm, D_MODEL), lambda b, t: (b, t, 0))
    return pl.pallas_call(
        functools.partial(_out_proj_kernel, tm=tm, pad=pad),
        grid=(bsz, lp // tm),
        in_specs=[row, _full((D_MODEL, D_MODEL)), row],
        out_specs=row,
        out_shape=jax.ShapeDtypeStruct(h.shape, F32),
        compiler_params=_params(2),
        name="out_proj",
    )(a, w, h)


def _hgrn_kernel(h_ref, gain_ref, win_ref, bf_ref, lb_ref, go_ref, wout_ref, tri_ref, o_ref,
                 state_ref, *, pad, n_sub):
    t = pl.program_id(1)

    @pl.when(t == 0)
    def _():
        state_ref[...] = jnp.zeros_like(state_ref)

    x = h_ref[0]
    proj = _dot(_rms(x, gain_ref[...]).astype(BF16), win_ref[...])
    tri = tri_ref[...]
    row = lax.broadcasted_iota(jnp.int32, (HG_CHUNK, HG_KEY_TOTAL), 0)
    causal = (lax.broadcasted_iota(jnp.int32, (HG_CHUNK, HG_CHUNK), 1)
              <= lax.broadcasted_iota(jnp.int32, (HG_CHUNK, HG_CHUNK), 0))
    heads = [slice(hd * HG_KEY_DIM, (hd + 1) * HG_KEY_DIM) for hd in range(HG_HEADS)]

    q_all = _silu(proj[:, :HG_KEY_TOTAL])
    lb = lb_ref[...]
    f = lb + (1.0 - lb) * _sigmoid(proj[:, HG_KEY_TOTAL:2 * HG_KEY_TOTAL] + bf_ref[...])
    kk_all = 1.0 - f
    v_all = proj[:, 2 * HG_KEY_TOTAL:2 * HG_KEY_TOTAL + D_MODEL]
    lf_hi_all, lf_lo_all = _split2(jnp.log(f))
    states = [state_ref[hd] for hd in range(HG_HEADS)]
    chunks = []
    for c in range(n_sub):
        rs = slice(c * HG_CHUNK, (c + 1) * HG_CHUNK)
        q, kk, v = q_all[rs], kk_all[rs], v_all[rs]
        b = _dot(tri, lf_hi_all[rs]) + _dot(tri, lf_lo_all[rs])
        q_dec = q * jnp.exp(b)
        outs = [_dot_nt(q_dec[:, sl], states[hd]) for hd, sl in enumerate(heads)]
        blocks = [[] for _ in heads]
        for sc in range(HG_CHUNK // HG_SUB):
            lo_r, hi_r = sc * HG_SUB, (sc + 1) * HG_SUB
            ref = b[lo_r - 1:lo_r, :] if sc else jnp.zeros((1, HG_KEY_TOTAL), F32)
            qt = q[lo_r:hi_r] * jnp.exp(b[lo_r:hi_r] - ref)
            kt = kk * jnp.exp(jnp.where(row < hi_r, ref - b, NEG_BIG))
            for hd, sl in enumerate(heads):
                blocks[hd].append(_dot_nt(qt[:, sl], kt[:, sl]))
        b_end = b[HG_CHUNK - 1:HG_CHUNK, :]
        k_dec = kk * jnp.exp(b_end - b)
        s_dec = jnp.exp(b_end)
        for hd, sl in enumerate(heads):
            scores = jnp.where(causal, jnp.concatenate(blocks[hd], axis=0), 0.0)
            outs[hd] = _rms(outs[hd] + _dot(scores, v[:, sl]), go_ref[...])
            states[hd] = s_dec[:, sl] * states[hd] + _dot_tn(v[:, sl], k_dec[:, sl])
        chunks.append(jnp.concatenate(outs, axis=-1))
    for hd in range(HG_HEADS):
        state_ref[hd] = states[hd]
    gate = _silu(proj[:, 2 * HG_KEY_TOTAL + D_MODEL:])
    o_all = jnp.concatenate(chunks, axis=0)
    y = x + _dot((o_all * gate).astype(BF16), wout_ref[...])
    pos = t * (n_sub * HG_CHUNK) + lax.broadcasted_iota(jnp.int32, y.shape, 0)
    o_ref[0] = jnp.where(pos >= pad, y, 0.0)


def _hgrn_layer(h, gain, win, bf, lb, go, wout, pad):
    bsz, lp, _ = h.shape
    hg_in = win.shape[1]
    n_sub = _row_tile(lp) // HG_CHUNK
    tri = jnp.tril(jnp.ones((HG_CHUNK, HG_CHUNK), BF16))
    row = pl.BlockSpec((1, n_sub * HG_CHUNK, D_MODEL), lambda b, t: (b, t, 0))
    return pl.pallas_call(
        functools.partial(_hgrn_kernel, pad=pad, n_sub=n_sub),
        grid=(bsz, lp // (n_sub * HG_CHUNK)),
        in_specs=[row, _full((1, D_MODEL)), _full((D_MODEL, hg_in)), _full((1, HG_KEY_TOTAL)),
                  _full((1, HG_KEY_TOTAL)), _full((1, HG_VAL_DIM)), _full((D_MODEL, D_MODEL)),
                  _full((HG_CHUNK, HG_CHUNK))],
        out_specs=row,
        out_shape=jax.ShapeDtypeStruct(h.shape, F32),
        scratch_shapes=[pltpu.VMEM((HG_HEADS, HG_VAL_DIM, HG_KEY_DIM), F32)],
        compiler_params=_params(2),
        name="hgrn_layer",
    )(h, gain, win, bf, lb, go, wout, tri)


def _router_kernel(h_ref, gain_ref, w_ref, b_ref, tri_ref, idx_ref, gate_ref, xn_ref, count_ref):
    @pl.when(pl.program_id(0) == 0)
    def _():
        count_ref[...] = jnp.zeros_like(count_ref)

    xn = _rms(h_ref[...], gain_ref[...])
    xn_ref[...] = xn
    x_hi, x_lo = _split2(xn)
    w_hi, w_lo = _split2(w_ref[...])
    logit = _dot(x_hi, w_hi) + _dot(x_hi, w_lo) + _dot(x_lo, w_hi) + b_ref[...]
    lane = lax.broadcasted_iota(jnp.int32, logit.shape, 1)
    is_grp = (lane >= N_EXPERTS) & (lane < N_EXPERTS + N_GROUPS)
    g_logit = jnp.where(is_grp, logit, -jnp.inf)
    g_max = jnp.max(g_logit, axis=-1, keepdims=True)
    g_idx = jnp.min(jnp.where(g_logit == g_max, lane, 4 * LANES), axis=-1, keepdims=True) - N_EXPERTS
    g_gate = 1.0 / jnp.sum(jnp.exp(g_logit - g_max), axis=-1, keepdims=True)
    in_grp = (lane >= g_idx * EXPERTS_PER_GROUP) & (lane < (g_idx + 1) * EXPERTS_PER_GROUP)
    e1 = jnp.where(in_grp, logit, -jnp.inf)
    v1 = jnp.max(e1, axis=-1, keepdims=True)
    i1 = jnp.min(jnp.where(e1 == v1, lane, 4 * LANES), axis=-1, keepdims=True)
    e2 = jnp.where(lane == i1, -jnp.inf, e1)
    v2 = jnp.max(e2, axis=-1, keepdims=True)
    i2 = jnp.min(jnp.where(e2 == v2, lane, 4 * LANES), axis=-1, keepdims=True)
    z = jnp.exp(v2 - v1)
    w1 = g_gate / (1.0 + z)
    gate_ref[...] = jnp.where(lane == 0, w1, jnp.where(lane == 1, w1 * z, 0.0))
    pick1 = lane == i1
    pick2 = lane == i2
    onehot = jnp.where(pick1 | pick2, 1.0, 0.0)
    before = count_ref[...] + _dot(tri_ref[...], onehot.astype(BF16))
    r1 = jnp.sum(jnp.where(pick1, before, 0.0), axis=-1, keepdims=True).astype(jnp.int32)
    r2 = jnp.sum(jnp.where(pick2, before, 0.0), axis=-1, keepdims=True).astype(jnp.int32)
    tm = onehot.shape[0]
    count_ref[...] = before[tm - 1:tm, :] + onehot[tm - 1:tm, :]
    table = jnp.where(lane == 0, i1, jnp.where(lane == 1, i2,
                      jnp.where(lane == 2, r1, jnp.where(lane == 3, r2, 0)))).astype(F32)
    idx_ref[...] = table.T[:SUBLANES, :].astype(jnp.int32)


def _router(h2, gain, w_cat, b_cat):
    n = h2.shape[0]
    tm = _row_tile(n)
    row = pl.BlockSpec((tm, D_MODEL), lambda t: (t, 0))
    lanes = pl.BlockSpec((tm, LANES), lambda t: (t, 0))
    tri = jnp.tril(jnp.ones((tm, tm), BF16), -1)
    return pl.pallas_call(
        _router_kernel,
        grid=(n // tm,),
        in_specs=[row, _full((1, D_MODEL)), _full((D_MODEL, LANES)), _full((1, LANES)), _full((tm, tm))],
        out_specs=[pl.BlockSpec((SUBLANES, tm), lambda t: (0, t)), lanes, row,
                   _full((1, LANES))],
        out_shape=[jax.ShapeDtypeStruct((SUBLANES, n), jnp.int32),
                   jax.ShapeDtypeStruct((n, LANES), F32),
                   jax.ShapeDtypeStruct((n, D_MODEL), F32),
                   jax.ShapeDtypeStruct((1, LANES), F32)],
        compiler_params=_params(1),
        name="moe_router",
    )(h2, gain, w_cat, b_cat, tri)


def _row_copy(src_ref, dst_ref, sem, src_row, dst_row):
    return pltpu.make_async_copy(src_ref.at[pl.ds(src_row, 1)], dst_ref.at[pl.ds(dst_row, 1)], sem)


def _dispatch_kernel(fill_ref, nb_ref, d0_ref, d1_ref, x_ref, o_hbm, zero_ref, sem, *, tm, n_blocks):
    @pl.when(pl.program_id(0) == 0)
    def _():
        zero_ref[...] = jnp.zeros_like(zero_ref)

        def fill_copy(row):
            return pltpu.make_async_copy(zero_ref, o_hbm.at[pl.ds(row, EXPERT_ROWS)], sem.at[2])

        def tail_start(b, carry):
            fill_copy(pl.multiple_of(b * EXPERT_ROWS, EXPERT_ROWS)).start()
            return carry

        def tail_wait(b, carry):
            fill_copy(pl.multiple_of(b * EXPERT_ROWS, EXPERT_ROWS)).wait()
            return carry

        for e in range(N_EXPERTS):
            fill_copy(pl.multiple_of(fill_ref[e], SUBLANES)).start()
        for e in range(N_EXPERTS):
            fill_copy(pl.multiple_of(fill_ref[e], SUBLANES)).wait()
        lax.fori_loop(nb_ref[0], n_blocks + 1, tail_start, 0)
        lax.fori_loop(nb_ref[0], n_blocks + 1, tail_wait, 0)

    def issue(r, carry):
        _row_copy(x_ref, o_hbm, sem.at[0], r, d0_ref[0, 0, r]).start(priority=0)
        _row_copy(x_ref, o_hbm, sem.at[1], r, d1_ref[0, 0, r]).start(priority=1)
        return carry

    lax.fori_loop(0, tm, issue, 0)
    pltpu.make_async_copy(x_ref, o_hbm.at[pl.ds(0, tm)], sem.at[0]).wait()
    pltpu.make_async_copy(x_ref, o_hbm.at[pl.ds(0, tm)], sem.at[1]).wait()


def _dispatch(xn3, dest, fill_start, n_used, n_blocks):
    n = xn3.shape[0]
    tm = 2 * _row_tile(n) if n % (2 * _row_tile(n)) == 0 else _row_tile(n)
    steps = n // tm
    idx = pl.BlockSpec((1, 1, tm), lambda t, fill, nb: (t, 0, 0), memory_space=pltpu.SMEM)
    return pl.pallas_call(
        functools.partial(_dispatch_kernel, tm=tm, n_blocks=n_blocks),
        grid_spec=pltpu.PrefetchScalarGridSpec(
            num_scalar_prefetch=2,
            grid=(steps,),
            in_specs=[idx, idx, pl.BlockSpec((tm, D_MODEL), lambda t, fill, nb: (t, 0))],
            out_specs=pl.BlockSpec(memory_space=pl.ANY),
            scratch_shapes=[pltpu.VMEM((EXPERT_ROWS, D_MODEL), F32),
                            pltpu.SemaphoreType.DMA((3,))]),
        out_shape=jax.ShapeDtypeStruct(((n_blocks + 1) * EXPERT_ROWS, D_MODEL), F32),
        compiler_params=_params(1),
        name="moe_dispatch",
    )(fill_start, n_used, dest[0].reshape(steps, 1, tm), dest[1].reshape(steps, 1, tm), xn3)


def _expert_kernel(be_ref, nb_ref, x_ref, wup_ref, wdn_ref, y_ref):
    del be_ref

    used = pl.program_id(0) < nb_ref[0]

    @pl.when(jnp.logical_not(used))
    def _():
        y_ref[...] = jnp.zeros_like(y_ref)

    @pl.when(used)
    def _():
        gu = _dot(x_ref[...], wup_ref[0])
        act = _silu(gu[:, :D_EXPERT]) * gu[:, D_EXPERT:]
        y_ref[...] = _dot(act, wdn_ref[0])


def _experts(x_pad3, block_expert, n_used, w_up, w_down, n_blocks):
    blk = pl.BlockSpec((EXPERT_ROWS, D_MODEL), lambda i, be, nb: (i, 0))
    return pl.pallas_call(
        _expert_kernel,
        grid_spec=pltpu.PrefetchScalarGridSpec(
            num_scalar_prefetch=2,
            grid=(n_blocks,),
            in_specs=[blk,
                      pl.BlockSpec((1, D_MODEL, 2 * D_EXPERT), lambda i, be, nb: (be[i], 0, 0)),
                      pl.BlockSpec((1, D_EXPERT, D_MODEL), lambda i, be, nb: (be[i], 0, 0))],
            out_specs=blk),
        out_shape=jax.ShapeDtypeStruct((n_blocks * EXPERT_ROWS, D_MODEL), F32),
        compiler_params=_params(1),
        name="moe_experts",
    )(block_expert, n_used, x_pad3, w_up, w_down)


def _combine_kernel(d0_ref, d1_ref, d0n_ref, d1n_ref, y_hbm, h_ref, gate_ref, o_ref, buf, sem,
                    *, tm, steps, lp, pad):
    t = pl.program_id(0)
    slot = t % 2

    def fetch(i0_ref, i1_ref, into):
        def issue(r, carry):
            _row_copy(y_hbm, buf.at[2 * into], sem.at[2 * into], i0_ref[0, 0, r], r).start(priority=0)
            _row_copy(y_hbm, buf.at[2 * into + 1], sem.at[2 * into + 1], i1_ref[0, 0, r], r).start(priority=1)
            return carry

        lax.fori_loop(0, tm, issue, 0)

    @pl.when(t == 0)
    def _():
        fetch(d0_ref, d1_ref, slot)

    @pl.when(t + 1 < steps)
    def _():
        fetch(d0n_ref, d1n_ref, 1 - slot)

    for k in range(TOP_K):
        pltpu.make_async_copy(y_hbm.at[pl.ds(0, tm)], buf.at[2 * slot + k], sem.at[2 * slot + k]).wait()
    g0 = gate_ref[:, 0:1]
    g1 = gate_ref[:, 1:2]
    pos = (t * tm + lax.broadcasted_iota(jnp.int32, (tm, D_MODEL), 0)) % lp
    y = h_ref[...] + g0 * buf[2 * slot] + g1 * buf[2 * slot + 1]
    o_ref[...] = jnp.where(pos >= pad, y, 0.0)


def _combine(y3, dest, gate, h2, lp, pad):
    n = h2.shape[0]
    tm = _row_tile(lp)
    steps = n // tm
    idx = pl.BlockSpec((1, 1, tm), lambda t: (t, 0, 0), memory_space=pltpu.SMEM)
    idx_next = pl.BlockSpec((1, 1, tm), lambda t: (jnp.minimum(t + 1, steps - 1), 0, 0),
                            memory_space=pltpu.SMEM)
    row = pl.BlockSpec((tm, D_MODEL), lambda t: (t, 0))
    d0 = dest[0].reshape(steps, 1, tm)
    d1 = dest[1].reshape(steps, 1, tm)
    return pl.pallas_call(
        functools.partial(_combine_kernel, tm=tm, steps=steps, lp=lp, pad=pad),
        grid=(steps,),
        in_specs=[idx, idx, idx_next, idx_next, pl.BlockSpec(memory_space=pl.ANY), row,
                  pl.BlockSpec((tm, LANES), lambda t: (t, 0))],
        out_specs=row,
        out_shape=jax.ShapeDtypeStruct(h2.shape, F32),
        scratch_shapes=[pltpu.VMEM((2 * TOP_K, tm, D_MODEL), F32),
                        pltpu.SemaphoreType.DMA((2 * TOP_K,))],
        compiler_params=_params(1),
        name="moe_combine",
    )(d0, d1, d0, d1, y3, h2, gate)


def _moe_layer(h, gain, w_grp, b_grp, w_rt, b_rt, w_up, w_down, pad):
    bsz, lp, _ = h.shape
    n = bsz * lp
    h2 = h.reshape(n, D_MODEL)
    fill = LANES - N_EXPERTS - N_GROUPS
    w_cat = jnp.concatenate([w_rt, w_grp, jnp.zeros((D_MODEL, fill), F32)], axis=1)
    b_cat = jnp.concatenate([b_rt, b_grp, jnp.zeros((fill,), F32)])[None, :]
    idx, gate, xn3, counts = _router(h2, gain, w_cat, b_cat)

    counts = counts[0, :N_EXPERTS].astype(jnp.int32)
    padded = (counts + EXPERT_ROWS - 1) // EXPERT_ROWS * EXPERT_ROWS
    pend = jnp.cumsum(padded)
    pstart = pend - padded
    dest = pstart[idx[:TOP_K]] + idx[TOP_K:2 * TOP_K]
    n_blocks = (n * TOP_K + N_EXPERTS * (EXPERT_ROWS - 1) + EXPERT_ROWS - 1) // EXPERT_ROWS
    block_start = jnp.arange(n_blocks, dtype=jnp.int32) * EXPERT_ROWS
    block_expert = jnp.minimum(jnp.sum(block_start[:, None] >= pend[None, :], axis=1),
                               N_EXPERTS - 1).astype(jnp.int32)
    n_used = (pend[N_EXPERTS - 1:] // EXPERT_ROWS).astype(jnp.int32)

    fill_start = (pstart + counts) // SUBLANES * SUBLANES
    x_pad3 = _dispatch(xn3, dest, fill_start, n_used, n_blocks)
    y3 = _experts(x_pad3, block_expert, n_used, w_up, w_down, n_blocks)
    return _combine(y3, dest, gate, h2, lp, pad).reshape(bsz, lp, D_MODEL)


def _lower_bounds(lb_logits):
    p = jax.nn.softmax(lb_logits.astype(F32), axis=0)
    cum = jnp.cumsum(p, axis=0)
    return cum - cum[:1]


@jax.jit
def kernel(x, meta_tokens, fox_norm, fox_w_in, fox_b_f, fox_q_norm, fox_k_norm, fox_w_out, hg_norm, hg_w_in, hg_b_f, hg_lb_logits, hg_o_norm, hg_w_out, moe_norm, moe_w_grp, moe_b_grp, moe_w_rt, moe_b_rt, moe_w_up, moe_w_down):
    bsz, seq, _ = x.shape
    depth = moe_norm.shape[0]
    length = N_META + seq
    pad = (-length) % ROW_ALIGN
    meta = jnp.broadcast_to(meta_tokens[None].astype(x.dtype), (bsz, N_META, D_MODEL))
    h = jnp.concatenate([jnp.zeros((bsz, pad, D_MODEL), x.dtype), meta, x], axis=1)
    lower_bounds = _lower_bounds(hg_lb_logits)
    scale = FOX_HEAD_DIM ** -0.5
    for i in range(depth):
        j = i // 2
        if i % 2 == 0:
            shift = FOX_HEAD_DIM * scale * jnp.max(jnp.abs(fox_q_norm[j])) * jnp.max(jnp.abs(fox_k_norm[j]))
            qa, ka, va = _fox_in(
                h, fox_norm[j][None, :], fox_w_in[j][:, :3 * D_MODEL].astype(BF16),
                fox_w_in[j][:, 3 * D_MODEL:], fox_b_f[j][None, :],
                jnp.tile(fox_q_norm[j] * (scale * LOG2E), FOX_HEADS)[None, :],
                jnp.tile(fox_k_norm[j], FOX_HEADS)[None, :], shift, pad)
            o = _flash(qa, ka, va, h.shape[1])
            h = _out_proj(o, fox_w_out[j].astype(BF16), h, pad)
        else:
            h = _hgrn_layer(h, hg_norm[j][None, :], hg_w_in[j].astype(BF16), hg_b_f[j][None, :],
                            lower_bounds[i][None, :], hg_o_norm[j][None, :],
                            hg_w_out[j].astype(BF16), pad)
        h = _moe_layer(h, moe_norm[i][None, :], moe_w_grp[i], moe_b_grp[i], moe_w_rt[i], moe_b_rt[i],
                       moe_w_up[i], moe_w_down[i], pad)
    return h[:, pad + N_META:]
```

```python
import functools

import jax
import jax.numpy as jnp
from jax import lax
from jax.experimental import pallas as pl
from jax.experimental.pallas import tpu as pltpu

F32 = jnp.float32
BF16 = jnp.bfloat16
HIGHEST = lax.Precision.HIGHEST

D_MODEL = 1024
N_META = 16
ROW_ALIGN = 128
EPS = 1e-6
NEG_BIG = -1e30

FOX_HEADS = 16
FOX_HEAD_DIM = D_MODEL // FOX_HEADS
HG_HEADS = 8
HG_KEY_DIM = 128
HG_VAL_DIM = D_MODEL // HG_HEADS
HG_KEY_TOTAL = HG_HEADS * HG_KEY_DIM
HG_CHUNK = 128
HG_SUB = 16

N_GROUPS = 4
EXPERTS_PER_GROUP = 8
N_EXPERTS = N_GROUPS * EXPERTS_PER_GROUP
TOP_K = 2
D_EXPERT = D_MODEL // 2
EXPERT_ROWS = 256

LANES = 128
SUBLANES = 8
VMEM_LIMIT = 56 * 1024 * 1024


def _dot(a, b, precision=None):
    return jnp.dot(a, b, preferred_element_type=F32, precision=precision)


def _dot_nt(a, b):
    return lax.dot_general(a, b, (((1,), (1,)), ((), ())), preferred_element_type=F32)


def _dot_tn(a, b):
    return lax.dot_general(a, b, (((0,), (0,)), ((), ())), preferred_element_type=F32)


def _rms(x, gain):
    return x * lax.rsqrt(jnp.mean(x * x, axis=-1, keepdims=True) + EPS) * gain


def _sigmoid(x):
    return 1.0 / (1.0 + jnp.exp(-x))


def _silu(x):
    return x * _sigmoid(x)


def _row_tile(lp):
    for t in (384, 256, 128):
        if lp % t == 0:
            return t
    raise ValueError(lp)


def _params(n_axes):
    return pltpu.CompilerParams(dimension_semantics=("arbitrary",) * n_axes,
                                vmem_limit_bytes=VMEM_LIMIT)


def _full(shape):
    return pl.BlockSpec(shape, lambda *_: (0,) * len(shape))


ATT_BLOCK = 256
LOG2E = 1.4426950408889634

LANE_C_Q = FOX_HEAD_DIM
LANE_C_K = FOX_HEAD_DIM + 3
LANE_KEY_PAD = FOX_HEAD_DIM + 6
LANE_SHIFT = FOX_HEAD_DIM + 7
LANE_ONE_V = FOX_HEAD_DIM


def _bf16_trunc(x):
    bits = lax.bitcast_convert_type(x, jnp.int32) & jnp.int32(-65536)
    return lax.bitcast_convert_type(bits, F32)


def _split3(c):
    hi = _bf16_trunc(c)
    mid = _bf16_trunc(c - hi)
    return hi, mid, c - hi - mid


def _split2(x):
    hi = x.astype(BF16)
    return hi, (x - hi.astype(F32)).astype(BF16)


def _fox_in_kernel(h_ref, gain_ref, wqkv_ref, wf_ref, bf_ref, gq_ref, gk_ref, seg_ref, segt_ref,
                   tri_ref, rowq_ref, rowk_ref, qa_ref, ka_ref, va_ref, carry_ref, *, tm, nt, pad):
    t = pl.program_id(1)

    @pl.when(t == 0)
    def _():
        carry_ref[...] = jnp.zeros_like(carry_ref)

    @pl.when(t >= nt)
    def _():
        qa_ref[...] = jnp.zeros_like(qa_ref)
        ka_ref[...] = jnp.zeros_like(ka_ref)
        va_ref[...] = jnp.zeros_like(va_ref)

    @pl.when(t < nt)
    def _():
        xn = _rms(h_ref[0], gain_ref[...])
        qkv = _dot(xn.astype(BF16), wqkv_ref[...])

        def head_norm(z, g_ref):
            hi, lo = _split2(z * z)
            ms = (_dot(hi, seg_ref[...]) + _dot(lo, seg_ref[...])) * (1.0 / FOX_HEAD_DIM)
            hi, lo = _split2(lax.rsqrt(ms + EPS))
            return z * (_dot(hi, segt_ref[...]) + _dot(lo, segt_ref[...])) * g_ref[...]

        qn = head_norm(qkv[:, :D_MODEL], gq_ref)
        kn = head_norm(qkv[:, D_MODEL:2 * D_MODEL], gk_ref)
        vv = qkv[:, 2 * D_MODEL:]

        x_hi, x_lo = _split2(xn)
        w_hi, w_lo = _split2(wf_ref[...])
        fl = _dot(x_hi, w_hi) + _dot(x_hi, w_lo) + _dot(x_lo, w_hi) + bf_ref[...]
        log_f = jnp.minimum(fl, 0.0) - jnp.log(1.0 + jnp.exp(-jnp.abs(fl)))
        pos16 = t * tm + lax.broadcasted_iota(jnp.int32, log_f.shape, 0)
        log_f = jnp.where(pos16 >= pad, log_f, 0.0)
        tri = tri_ref[...]
        cum = carry_ref[...] + sum(_dot(tri, piece.astype(BF16)) for piece in _split3(log_f))
        carry_ref[...] = cum[tm - 1:tm, :]
        c_hi, c_mid, c_lo = _split3(cum * LOG2E)

        lane = lax.broadcasted_iota(jnp.int32, (tm, LANES), 1)
        pos = t * tm + lax.broadcasted_iota(jnp.int32, (tm, LANES), 0)
        is_head = lane < FOX_HEAD_DIM
        row_k = jnp.where((lane == LANE_KEY_PAD) & (pos < pad), NEG_BIG, rowk_ref[...])
        row_v = jnp.where(lane == LANE_ONE_V, 1.0, 0.0)
        for hd in range(FOX_HEADS):
            col = slice((hd // 2) * LANES, (hd // 2 + 1) * LANES)

            def head(z):
                z = z[:, col]
                return pltpu.roll(z, FOX_HEAD_DIM, axis=1) if hd % 2 else z

            def spread(piece):
                return jnp.broadcast_to(piece[:, hd:hd + 1], (tm, LANES))

            hi, mid, lo = spread(c_hi), spread(c_mid), spread(c_lo)
            aug_q = jnp.where(lane == LANE_C_Q, hi, jnp.where(lane == LANE_C_Q + 1, mid,
                              jnp.where(lane == LANE_C_Q + 2, lo, rowq_ref[...])))
            aug_k = jnp.where(lane == LANE_C_K, -hi, jnp.where(lane == LANE_C_K + 1, -mid,
                              jnp.where(lane == LANE_C_K + 2, -lo, row_k)))
            qa_ref[0, hd] = jnp.where(is_head, head(qn), aug_q).astype(BF16)
            ka_ref[0, hd] = jnp.where(is_head, head(kn), aug_k).astype(BF16)
            va_ref[0, hd] = jnp.where(is_head, head(vv), row_v).astype(BF16)


def _fox_in(h, gain, wqkv, wf, bf, gq, gk, shift, pad):
    bsz, lp, _ = h.shape
    tm = _row_tile(lp)
    nt = lp // tm
    lpa = -(-lp // ATT_BLOCK) * ATT_BLOCK
    head_of = jnp.arange(D_MODEL) // FOX_HEAD_DIM
    seg = (head_of[:, None] == jnp.arange(FOX_HEADS)[None, :]).astype(BF16)
    tri = jnp.tril(jnp.ones((tm, tm), BF16))
    lane = jnp.arange(LANES)
    row_q = jnp.where((lane >= LANE_C_K) & (lane <= LANE_KEY_PAD), 1.0,
                      jnp.where(lane == LANE_SHIFT, -shift * LOG2E, 0.0)).astype(F32)[None, :]
    row_k = (((lane >= LANE_C_Q) & (lane < LANE_C_K)) | (lane == LANE_SHIFT)).astype(F32)[None, :]
    out = pl.BlockSpec((1, FOX_HEADS, tm, LANES), lambda b, t: (b, 0, t, 0))
    return pl.pallas_call(
        functools.partial(_fox_in_kernel, tm=tm, nt=nt, pad=pad),
        grid=(bsz, -(-lpa // tm)),
        in_specs=[pl.BlockSpec((1, tm, D_MODEL), lambda b, t: (b, jnp.minimum(t, nt - 1), 0)),
                  _full((1, D_MODEL)), _full((D_MODEL, 3 * D_MODEL)),
                  _full((D_MODEL, FOX_HEADS)), _full((1, FOX_HEADS)),
                  _full((1, D_MODEL)), _full((1, D_MODEL)),
                  _full((D_MODEL, FOX_HEADS)), _full((FOX_HEADS, D_MODEL)), _full((tm, tm)),
                  _full((1, LANES)), _full((1, LANES))],
        out_specs=[out, out, out],
        out_shape=[jax.ShapeDtypeStruct((bsz, FOX_HEADS, lpa, LANES), BF16)] * 3,
        scratch_shapes=[pltpu.VMEM((1, FOX_HEADS), F32)],
        compiler_params=_params(2),
        name="fox_in",
    )(h, gain, wqkv, wf, bf, gq, gk, seg, seg.T, tri, row_q, row_k)


ATT_Q_STRIPS = 13


def _flash_kernel(q_ref, k_ref, v_ref, o_ref, acc_ref, *, tq, tk):
    i = pl.program_id(2)
    n_strip = tq // tk
    lane = lax.broadcasted_iota(jnp.int32, (tq, LANES), 1)

    for hd in range(2):
        q = q_ref[0, hd]

        def strip(rows, off, masked):
            s = _dot_nt(q[rows:], k_ref[0, hd, pl.ds(off, tk), :])
            if masked:
                r = lax.broadcasted_iota(jnp.int32, s.shape, 0)
                c = lax.broadcasted_iota(jnp.int32, s.shape, 1)
                s = jnp.where(c <= r, s, NEG_BIG)
            p = jnp.exp2(s).astype(BF16)
            return _dot(p, v_ref[0, hd, pl.ds(off, tk), :])

        def below_diagonal(j, carry):
            total = None
            for n in range(n_strip):
                part = strip(0, pl.multiple_of(j * tq + n * tk, tk), False)
                total = part if total is None else total + part
            acc_ref[...] += total
            return carry

        acc_ref[...] = jnp.zeros_like(acc_ref)
        lax.fori_loop(0, i, below_diagonal, 0)
        for n in range(n_strip):
            acc_ref[n * tk:, :] += strip(n * tk, pl.multiple_of(i * tq + n * tk, tk), True)
        acc = acc_ref[...]
        denom = acc[:, FOX_HEAD_DIM:FOX_HEAD_DIM + 1]
        out = acc / jnp.where(denom > 0.0, denom, 1.0)
        if hd == 0:
            o_ref[0] = out.astype(o_ref.dtype)
        else:
            shifted = pltpu.roll(out, FOX_HEAD_DIM, axis=1).astype(o_ref.dtype)
            o_ref[0] = jnp.where(lane < FOX_HEAD_DIM, o_ref[0], shifted)


def _flash(qa, ka, va, lp):
    bsz, nh, lpa, _ = qa.shape
    tk = ATT_BLOCK
    tq = ATT_Q_STRIPS * tk if lpa % (ATT_Q_STRIPS * tk) == 0 else tk
    qspec = pl.BlockSpec((1, 2, tq, LANES), lambda b, h, i: (b, h, i, 0))
    kvspec = pl.BlockSpec((1, 2, lpa, LANES), lambda b, h, i: (b, h, 0, 0))
    return pl.pallas_call(
        functools.partial(_flash_kernel, tq=tq, tk=tk),
        grid=(bsz, nh // 2, lpa // tq),
        in_specs=[qspec, kvspec, kvspec],
        out_specs=pl.BlockSpec((1, tq, LANES), lambda b, h, i: (b, i, h)),
        out_shape=jax.ShapeDtypeStruct((bsz, lp, D_MODEL), BF16),
        scratch_shapes=[pltpu.VMEM((tq, LANES), F32)],
        compiler_params=_params(3),
        name="fox_flash",
    )(qa, ka, va)


def _out_proj_kernel(a_ref, w_ref, h_ref, o_ref, *, tm, pad):
    t = pl.program_id(1)
    y = h_ref[0] + _dot(a_ref[0], w_ref[...])
    pos = t * tm + lax.broadcasted_iota(jnp.int32, y.shape, 0)
    o_ref[0] = jnp.where(pos >= pad, y, 0.0)


def _out_proj(a, w, h, pad):
    bsz, lp, _ = h.shape
    tm = _row_tile(lp)
    row = pl.BlockSpec((1, tm, D_MODEL), lambda b, t: (b, t, 0))
    return pl.pallas_call(
        functools.partial(_out_proj_kernel, tm=tm, pad=pad),
        grid=(bsz, lp // tm),
        in_specs=[row, _full((D_MODEL, D_MODEL)), row],
        out_specs=row,
        out_shape=jax.ShapeDtypeStruct(h.shape, F32),
        compiler_params=_params(2),
        name="out_proj",
    )(a, w, h)


def _hgrn_kernel(h_ref, gain_ref, win_ref, bf_ref, lb_ref, go_ref, wout_ref, tri_ref, o_ref,
                 state_ref, *, pad, n_sub):
    t = pl.program_id(1)

    @pl.when(t == 0)
    def _():
        state_ref[...] = jnp.zeros_like(state_ref)

    x = h_ref[0]
    proj = _dot(_rms(x, gain_ref[...]).astype(BF16), win_ref[...])
    tri = tri_ref[...]
    row = lax.broadcasted_iota(jnp.int32, (HG_CHUNK, HG_KEY_TOTAL), 0)
    causal = (lax.broadcasted_iota(jnp.int32, (HG_CHUNK, HG_CHUNK), 1)
              <= lax.broadcasted_iota(jnp.int32, (HG_CHUNK, HG_CHUNK), 0))
    heads = [slice(hd * HG_KEY_DIM, (hd + 1) * HG_KEY_DIM) for hd in range(HG_HEADS)]

    q_all = _silu(proj[:, :HG_KEY_TOTAL])
    lb = lb_ref[...]
    f = lb + (1.0 - lb) * _sigmoid(proj[:, HG_KEY_TOTAL:2 * HG_KEY_TOTAL] + bf_ref[...])
    kk_all = 1.0 - f
    v_all = proj[:, 2 * HG_KEY_TOTAL:2 * HG_KEY_TOTAL + D_MODEL]
    lf_hi_all, lf_lo_all = _split2(jnp.log(f))
    states = [state_ref[hd] for hd in range(HG_HEADS)]
    chunks = []
    for c in range(n_sub):
        rs = slice(c * HG_CHUNK, (c + 1) * HG_CHUNK)
        q, kk, v = q_all[rs], kk_all[rs], v_all[rs]
        b = _dot(tri, lf_hi_all[rs]) + _dot(tri, lf_lo_all[rs])
        q_dec = q * jnp.exp(b)
        outs = [_dot_nt(q_dec[:, sl], states[hd]) for hd, sl in enumerate(heads)]
        blocks = [[] for _ in heads]
        for sc in range(HG_CHUNK // HG_SUB):
            lo_r, hi_r = sc * HG_SUB, (sc + 1) * HG_SUB
            ref = b[lo_r - 1:lo_r, :] if sc else jnp.zeros((1, HG_KEY_TOTAL), F32)
            qt = q[lo_r:hi_r] * jnp.exp(b[lo_r:hi_r] - ref)
            kt = kk * jnp.exp(jnp.where(row < hi_r, ref - b, NEG_BIG))
            for hd, sl in enumerate(heads):
                blocks[hd].append(_dot_nt(qt[:, sl], kt[:, sl]))
        b_end = b[HG_CHUNK - 1:HG_CHUNK, :]
        k_dec = kk * jnp.exp(b_end - b)
        s_dec = jnp.exp(b_end)
        for hd, sl in enumerate(heads):
            scores = jnp.where(causal, jnp.concatenate(blocks[hd], axis=0), 0.0)
            outs[hd] = _rms(outs[hd] + _dot(scores, v[:, sl]), go_ref[...])
            states[hd] = s_dec[:, sl] * states[hd] + _dot_tn(v[:, sl], k_dec[:, sl])
        chunks.append(jnp.concatenate(outs, axis=-1))
    for hd in range(HG_HEADS):
        state_ref[hd] = states[hd]
    gate = _silu(proj[:, 2 * HG_KEY_TOTAL + D_MODEL:])
    o_all = jnp.concatenate(chunks, axis=0)
    y = x + _dot((o_all * gate).astype(BF16), wout_ref[...])
    pos = t * (n_sub * HG_CHUNK) + lax.broadcasted_iota(jnp.int32, y.shape, 0)
    o_ref[0] = jnp.where(pos >= pad, y, 0.0)


def _hgrn_layer(h, gain, win, bf, lb, go, wout, pad):
    bsz, lp, _ = h.shape
    hg_in = win.shape[1]
    n_sub = _row_tile(lp) // HG_CHUNK
    tri = jnp.tril(jnp.ones((HG_CHUNK, HG_CHUNK), BF16))
    row = pl.BlockSpec((1, n_sub * HG_CHUNK, D_MODEL), lambda b, t: (b, t, 0))
    return pl.pallas_call(
        functools.partial(_hgrn_kernel, pad=pad, n_sub=n_sub),
        grid=(bsz, lp // (n_sub * HG_CHUNK)),
        in_specs=[row, _full((1, D_MODEL)), _full((D_MODEL, hg_in)), _full((1, HG_KEY_TOTAL)),
                  _full((1, HG_KEY_TOTAL)), _full((1, HG_VAL_DIM)), _full((D_MODEL, D_MODEL)),
                  _full((HG_CHUNK, HG_CHUNK))],
        out_specs=row,
        out_shape=jax.ShapeDtypeStruct(h.shape, F32),
        scratch_shapes=[pltpu.VMEM((HG_HEADS, HG_VAL_DIM, HG_KEY_DIM), F32)],
        compiler_params=_params(2),
        name="hgrn_layer",
    )(h, gain, win, bf, lb, go, wout, tri)


def _router_kernel(h_ref, gain_ref, w_ref, b_ref, tri_ref, idx_ref, gate_ref, xn_ref, count_ref):
    @pl.when(pl.program_id(0) == 0)
    def _():
        count_ref[...] = jnp.zeros_like(count_ref)

    xn = _rms(h_ref[...], gain_ref[...])
    xn_ref[...] = xn
    x_hi, x_lo = _split2(xn)
    w_hi, w_lo = _split2(w_ref[...])
    logit = _dot(x_hi, w_hi) + _dot(x_hi, w_lo) + _dot(x_lo, w_hi) + b_ref[...]
    lane = lax.broadcasted_iota(jnp.int32, logit.shape, 1)
    is_grp = (lane >= N_EXPERTS) & (lane < N_EXPERTS + N_GROUPS)
    g_logit = jnp.where(is_grp, logit, -jnp.inf)
    g_max = jnp.max(g_logit, axis=-1, keepdims=True)
    g_idx = jnp.min(jnp.where(g_logit == g_max, lane, 4 * LANES), axis=-1, keepdims=True) - N_EXPERTS
    g_gate = 1.0 / jnp.sum(jnp.exp(g_logit - g_max), axis=-1, keepdims=True)
    in_grp = (lane >= g_idx * EXPERTS_PER_GROUP) & (lane < (g_idx + 1) * EXPERTS_PER_GROUP)
    e1 = jnp.where(in_grp, logit, -jnp.inf)
    v1 = jnp.max(e1, axis=-1, keepdims=True)
    i1 = jnp.min(jnp.where(e1 == v1, lane, 4 * LANES), axis=-1, keepdims=True)
    e2 = jnp.where(lane == i1, -jnp.inf, e1)
    v2 = jnp.max(e2, axis=-1, keepdims=True)
    i2 = jnp.min(jnp.where(e2 == v2, lane, 4 * LANES), axis=-1, keepdims=True)
    z = jnp.exp(v2 - v1)
    w1 = g_gate / (1.0 + z)
    gate_ref[...] = jnp.where(lane == 0, w1, jnp.where(lane == 1, w1 * z, 0.0))
    pick1 = lane == i1
    pick2 = lane == i2
    onehot = jnp.where(pick1 | pick2, 1.0, 0.0)
    before = count_ref[...] + _dot(tri_ref[...], onehot.astype(BF16))
    r1 = jnp.sum(jnp.where(pick1, before, 0.0), axis=-1, keepdims=True).astype(jnp.int32)
    r2 = jnp.sum(jnp.where(pick2, before, 0.0), axis=-1, keepdims=True).astype(jnp.int32)
    tm = onehot.shape[0]
    count_ref[...] = before[tm - 1:tm, :] + onehot[tm - 1:tm, :]
    table = jnp.where(lane == 0, i1, jnp.where(lane == 1, i2,
                      jnp.where(lane == 2, r1, jnp.where(lane == 3, r2, 0)))).astype(F32)
    idx_ref[...] = table.T[:SUBLANES, :].astype(jnp.int32)


def _router(h2, gain, w_cat, b_cat):
    n = h2.shape[0]
    tm = _row_tile(n)
    row = pl.BlockSpec((tm, D_MODEL), lambda t: (t, 0))
    lanes = pl.BlockSpec((tm, LANES), lambda t: (t, 0))
    tri = jnp.tril(jnp.ones((tm, tm), BF16), -1)
    return pl.pallas_call(
        _router_kernel,
        grid=(n // tm,),
        in_specs=[row, _full((1, D_MODEL)), _full((D_MODEL, LANES)), _full((1, LANES)), _full((tm, tm))],
        out_specs=[pl.BlockSpec((SUBLANES, tm), lambda t: (0, t)), lanes, row,
                   _full((1, LANES))],
        out_shape=[jax.ShapeDtypeStruct((SUBLANES, n), jnp.int32),
                   jax.ShapeDtypeStruct((n, LANES), F32),
                   jax.ShapeDtypeStruct((n, D_MODEL), F32),
                   jax.ShapeDtypeStruct((1, LANES), F32)],
        compiler_params=_params(1),
        name="moe_router",
    )(h2, gain, w_cat, b_cat, tri)


def _row_copy(src_ref, dst_ref, sem, src_row, dst_row):
    return pltpu.make_async_copy(src_ref.at[pl.ds(src_row, 1)], dst_ref.at[pl.ds(dst_row, 1)], sem)


def _dispatch_kernel(fill_ref, nb_ref, d0_ref, d1_ref, x_ref, o_hbm, zero_ref, sem, *, tm, n_blocks):
    @pl.when(pl.program_id(0) == 0)
    def _():
        zero_ref[...] = jnp.zeros_like(zero_ref)

        def fill_copy(row):
            return pltpu.make_async_copy(zero_ref, o_hbm.at[pl.ds(row, EXPERT_ROWS)], sem.at[2])

        def tail_start(b, carry):
            fill_copy(pl.multiple_of(b * EXPERT_ROWS, EXPERT_ROWS)).start()
            return carry

        def tail_wait(b, carry):
            fill_copy(pl.multiple_of(b * EXPERT_ROWS, EXPERT_ROWS)).wait()
            return carry

        for e in range(N_EXPERTS):
            fill_copy(pl.multiple_of(fill_ref[e], SUBLANES)).start()
        for e in range(N_EXPERTS):
            fill_copy(pl.multiple_of(fill_ref[e], SUBLANES)).wait()
        lax.fori_loop(nb_ref[0], n_blocks + 1, tail_start, 0)
        lax.fori_loop(nb_ref[0], n_blocks + 1, tail_wait, 0)

    def issue(r, carry):
        _row_copy(x_ref, o_hbm, sem.at[0], r, d0_ref[0, 0, r]).start()
        _row_copy(x_ref, o_hbm, sem.at[1], r, d1_ref[0, 0, r]).start()
        return carry

    lax.fori_loop(0, tm, issue, 0)
    pltpu.make_async_copy(x_ref, o_hbm.at[pl.ds(0, tm)], sem.at[0]).wait()
    pltpu.make_async_copy(x_ref, o_hbm.at[pl.ds(0, tm)], sem.at[1]).wait()


def _dispatch(xn3, dest, fill_start, n_used, n_blocks):
    n = xn3.shape[0]
    tm = 2 * _row_tile(n) if n % (2 * _row_tile(n)) == 0 else _row_tile(n)
    steps = n // tm
    idx = pl.BlockSpec((1, 1, tm), lambda t, fill, nb: (t, 0, 0), memory_space=pltpu.SMEM)
    return pl.pallas_call(
        functools.partial(_dispatch_kernel, tm=tm, n_blocks=n_blocks),
        grid_spec=pltpu.PrefetchScalarGridSpec(
            num_scalar_prefetch=2,
            grid=(steps,),
            in_specs=[idx, idx, pl.BlockSpec((tm, D_MODEL), lambda t, fill, nb: (t, 0))],
            out_specs=pl.BlockSpec(memory_space=pl.ANY),
            scratch_shapes=[pltpu.VMEM((EXPERT_ROWS, D_MODEL), F32),
                            pltpu.SemaphoreType.DMA((3,))]),
        out_shape=jax.ShapeDtypeStruct(((n_blocks + 1) * EXPERT_ROWS, D_MODEL), F32),
        compiler_params=_params(1),
        name="moe_dispatch",
    )(fill_start, n_used, dest[0].reshape(steps, 1, tm), dest[1].reshape(steps, 1, tm), xn3)


def _expert_kernel(be_ref, nb_ref, x_ref, wup_ref, wdn_ref, y_ref):
    del be_ref

    used = pl.program_id(0) < nb_ref[0]

    @pl.when(jnp.logical_not(used))
    def _():
        y_ref[...] = jnp.zeros_like(y_ref)

    @pl.when(used)
    def _():
        gu = _dot(x_ref[...], wup_ref[0])
        act = _silu(gu[:, :D_EXPERT]) * gu[:, D_EXPERT:]
        y_ref[...] = _dot(act, wdn_ref[0])


def _experts(x_pad3, block_expert, n_used, w_up, w_down, n_blocks):
    blk = pl.BlockSpec((EXPERT_ROWS, D_MODEL), lambda i, be, nb: (i, 0))
    return pl.pallas_call(
        _expert_kernel,
        grid_spec=pltpu.PrefetchScalarGridSpec(
            num_scalar_prefetch=2,
            grid=(n_blocks,),
            in_specs=[blk,
                      pl.BlockSpec((1, D_MODEL, 2 * D_EXPERT), lambda i, be, nb: (be[i], 0, 0)),
                      pl.BlockSpec((1, D_EXPERT, D_MODEL), lambda i, be, nb: (be[i], 0, 0))],
            out_specs=blk),
        out_shape=jax.ShapeDtypeStruct((n_blocks * EXPERT_ROWS, D_MODEL), F32),
        compiler_params=_params(1),
        name="moe_experts",
    )(block_expert, n_used, x_pad3, w_up, w_down)


def _combine_kernel(d0_ref, d1_ref, d0n_ref, d1n_ref, y_hbm, h_ref, gate_ref, o_ref, buf, sem,
                    *, tm, steps, lp, pad):
    t = pl.program_id(0)
    slot = t % 2

    def fetch(i0_ref, i1_ref, into):
        def issue(r, carry):
            _row_copy(y_hbm, buf.at[2 * into], sem.at[2 * into], i0_ref[0, 0, r], r).start()
            _row_copy(y_hbm, buf.at[2 * into + 1], sem.at[2 * into + 1], i1_ref[0, 0, r], r).start()
            return carry

        lax.fori_loop(0, tm, issue, 0)

    @pl.when(t == 0)
    def _():
        fetch(d0_ref, d1_ref, slot)

    @pl.when(t + 1 < steps)
    def _():
        fetch(d0n_ref, d1n_ref, 1 - slot)

    for k in range(TOP_K):
        pltpu.make_async_copy(y_hbm.at[pl.ds(0, tm)], buf.at[2 * slot + k], sem.at[2 * slot + k]).wait()
    g0 = gate_ref[:, 0:1]
    g1 = gate_ref[:, 1:2]
    pos = (t * tm + lax.broadcasted_iota(jnp.int32, (tm, D_MODEL), 0)) % lp
    y = h_ref[...] + g0 * buf[2 * slot] + g1 * buf[2 * slot + 1]
    o_ref[...] = jnp.where(pos >= pad, y, 0.0)


def _combine(y3, dest, gate, h2, lp, pad):
    n = h2.shape[0]
    tm = _row_tile(lp)
    steps = n // tm
    idx = pl.BlockSpec((1, 1, tm), lambda t: (t, 0, 0), memory_space=pltpu.SMEM)
    idx_next = pl.BlockSpec((1, 1, tm), lambda t: (jnp.minimum(t + 1, steps - 1), 0, 0),
                            memory_space=pltpu.SMEM)
    row = pl.BlockSpec((tm, D_MODEL), lambda t: (t, 0))
    d0 = dest[0].reshape(steps, 1, tm)
    d1 = dest[1].reshape(steps, 1, tm)
    return pl.pallas_call(
        functools.partial(_combine_kernel, tm=tm, steps=steps, lp=lp, pad=pad),
        grid=(steps,),
        in_specs=[idx, idx, idx_next, idx_next, pl.BlockSpec(memory_space=pl.ANY), row,
                  pl.BlockSpec((tm, LANES), lambda t: (t, 0))],
        out_specs=row,
        out_shape=jax.ShapeDtypeStruct(h2.shape, F32),
        scratch_shapes=[pltpu.VMEM((2 * TOP_K, tm, D_MODEL), F32),
                        pltpu.SemaphoreType.DMA((2 * TOP_K,))],
        compiler_params=_params(1),
        name="moe_combine",
    )(d0, d1, d0, d1, y3, h2, gate)


def _moe_layer(h, gain, w_grp, b_grp, w_rt, b_rt, w_up, w_down, layer, pad):
    bsz, lp, _ = h.shape
    n = bsz * lp
    h2 = h.reshape(n, D_MODEL)
    fill = LANES - N_EXPERTS - N_GROUPS
    w_cat = jnp.concatenate([w_rt, w_grp, jnp.zeros((D_MODEL, fill), F32)], axis=1)
    b_cat = jnp.concatenate([b_rt, b_grp, jnp.zeros((fill,), F32)])[None, :]
    idx, gate, xn3, counts = _router(h2, gain, w_cat, b_cat)

    counts = counts[0, :N_EXPERTS].astype(jnp.int32)
    padded = (counts + EXPERT_ROWS - 1) // EXPERT_ROWS * EXPERT_ROWS
    pend = jnp.cumsum(padded)
    pstart = pend - padded
    seg_start = jnp.sum(jnp.where(idx[:TOP_K, :, None] == jnp.arange(N_EXPERTS), pstart, 0), axis=-1)
    dest = seg_start + idx[TOP_K:2 * TOP_K]
    n_blocks = (n * TOP_K + N_EXPERTS * (EXPERT_ROWS - 1) + EXPERT_ROWS - 1) // EXPERT_ROWS
    block_start = jnp.arange(n_blocks, dtype=jnp.int32) * EXPERT_ROWS
    block_expert = jnp.minimum(jnp.sum(block_start[:, None] >= pend[None, :], axis=1),
                               N_EXPERTS - 1).astype(jnp.int32) + layer * N_EXPERTS
    n_used = (pend[N_EXPERTS - 1:] // EXPERT_ROWS).astype(jnp.int32)

    fill_start = (pstart + counts) // SUBLANES * SUBLANES
    x_pad3 = _dispatch(xn3, dest, fill_start, n_used, n_blocks)
    y3 = _experts(x_pad3, block_expert, n_used, w_up, w_down, n_blocks)
    return _combine(y3, dest, gate, h2, lp, pad).reshape(bsz, lp, D_MODEL)


def _lower_bounds(lb_logits):
    p = jax.nn.softmax(lb_logits.astype(F32), axis=0)
    cum = jnp.cumsum(p, axis=0)
    return cum - cum[:1]


@jax.jit
def kernel(x, meta_tokens, fox_norm, fox_w_in, fox_b_f, fox_q_norm, fox_k_norm, fox_w_out, hg_norm, hg_w_in, hg_b_f, hg_lb_logits, hg_o_norm, hg_w_out, moe_norm, moe_w_grp, moe_b_grp, moe_w_rt, moe_b_rt, moe_w_up, moe_w_down):
    bsz, seq, _ = x.shape
    depth = moe_norm.shape[0]
    length = N_META + seq
    pad = (-length) % ROW_ALIGN
    meta = jnp.broadcast_to(meta_tokens[None].astype(x.dtype), (bsz, N_META, D_MODEL))
    h = jnp.concatenate([jnp.zeros((bsz, pad, D_MODEL), x.dtype), meta, x], axis=1)
    lower_bounds = _lower_bounds(hg_lb_logits)
    scale = FOX_HEAD_DIM ** -0.5
    for i in range(depth):
        j = i // 2
        if i % 2 == 0:
            shift = FOX_HEAD_DIM * scale * jnp.max(jnp.abs(fox_q_norm[j])) * jnp.max(jnp.abs(fox_k_norm[j]))
            qa, ka, va = _fox_in(
                h, fox_norm[j][None, :], fox_w_in[j][:, :3 * D_MODEL].astype(BF16),
                fox_w_in[j][:, 3 * D_MODEL:], fox_b_f[j][None, :],
                jnp.tile(fox_q_norm[j] * (scale * LOG2E), FOX_HEADS)[None, :],
                jnp.tile(fox_k_norm[j], FOX_HEADS)[None, :], shift, pad)
            o = _flash(qa, ka, va, h.shape[1])
            h = _out_proj(o, fox_w_out[j].astype(BF16), h, pad)
        else:
            h = _hgrn_layer(h, hg_norm[j][None, :], hg_w_in[j].astype(BF16), hg_b_f[j][None, :],
                            lower_bounds[i][None, :], hg_o_norm[j][None, :],
                            hg_w_out[j].astype(BF16), pad)
        h = _moe_layer(h, moe_norm[i][None, :], moe_w_grp[i], moe_b_grp[i], moe_w_rt[i], moe_b_rt[i],
                       moe_w_up.reshape(depth * N_EXPERTS, D_MODEL, 2 * D_EXPERT),
                       moe_w_down.reshape(depth * N_EXPERTS, D_EXPERT, D_MODEL), i, pad)
    return h[:, pad + N_META:]
```

```python
import functools

import jax
import jax.numpy as jnp
from jax import lax
from jax.experimental import pallas as pl
from jax.experimental.pallas import tpu as pltpu

F32 = jnp.float32
BF16 = jnp.bfloat16
HIGHEST = lax.Precision.HIGHEST

D_MODEL = 1024
N_META = 16
ROW_ALIGN = 128
EPS = 1e-6
NEG_BIG = -1e30

FOX_HEADS = 16
FOX_HEAD_DIM = D_MODEL // FOX_HEADS
HG_HEADS = 8
HG_KEY_DIM = 128
HG_VAL_DIM = D_MODEL // HG_HEADS
HG_KEY_TOTAL = HG_HEADS * HG_KEY_DIM
HG_CHUNK = 128
HG_SUB = 16

N_GROUPS = 4
EXPERTS_PER_GROUP = 8
N_EXPERTS = N_GROUPS * EXPERTS_PER_GROUP
TOP_K = 2
D_EXPERT = D_MODEL // 2
EXPERT_ROWS = 256

LANES = 128
SUBLANES = 8
VMEM_LIMIT = 56 * 1024 * 1024


def _dot(a, b, precision=None):
    return jnp.dot(a, b, preferred_element_type=F32, precision=precision)


def _dot_nt(a, b):
    return lax.dot_general(a, b, (((1,), (1,)), ((), ())), preferred_element_type=F32)


def _dot_tn(a, b):
    return lax.dot_general(a, b, (((0,), (0,)), ((), ())), preferred_element_type=F32)


def _rms(x, gain):
    return x * lax.rsqrt(jnp.mean(x * x, axis=-1, keepdims=True) + EPS) * gain


def _sigmoid(x):
    return 1.0 / (1.0 + jnp.exp(-x))


def _silu(x):
    return x * _sigmoid(x)


def _row_tile(lp):
    for t in (384, 256, 128):
        if lp % t == 0:
            return t
    raise ValueError(lp)


def _params(n_axes):
    return pltpu.CompilerParams(dimension_semantics=("arbitrary",) * n_axes,
                                vmem_limit_bytes=VMEM_LIMIT)


def _full(shape):
    return pl.BlockSpec(shape, lambda *_: (0,) * len(shape))


ATT_BLOCK = 256
LOG2E = 1.4426950408889634

LANE_C_Q = FOX_HEAD_DIM
LANE_C_K = FOX_HEAD_DIM + 3
LANE_KEY_PAD = FOX_HEAD_DIM + 6
LANE_SHIFT = FOX_HEAD_DIM + 7
LANE_ONE_V = FOX_HEAD_DIM


def _bf16_trunc(x):
    bits = lax.bitcast_convert_type(x, jnp.int32) & jnp.int32(-65536)
    return lax.bitcast_convert_type(bits, F32)


def _split3(c):
    hi = _bf16_trunc(c)
    mid = _bf16_trunc(c - hi)
    return hi, mid, c - hi - mid


def _split2(x):
    hi = x.astype(BF16)
    return hi, (x - hi.astype(F32)).astype(BF16)


def _fox_in_kernel(h_ref, gain_ref, wqkv_ref, wf_ref, bf_ref, gq_ref, gk_ref, seg_ref, segt_ref,
                   tri_ref, rowq_ref, rowk_ref, qa_ref, ka_ref, va_ref, carry_ref, *, tm, nt, pad):
    t = pl.program_id(1)

    @pl.when(t == 0)
    def _():
        carry_ref[...] = jnp.zeros_like(carry_ref)

    @pl.when(t >= nt)
    def _():
        qa_ref[...] = jnp.zeros_like(qa_ref)
        ka_ref[...] = jnp.zeros_like(ka_ref)
        va_ref[...] = jnp.zeros_like(va_ref)

    @pl.when(t < nt)
    def _():
        xn = _rms(h_ref[0], gain_ref[...])
        qkv = _dot(xn.astype(BF16), wqkv_ref[...])

        def head_norm(z, g_ref):
            hi, lo = _split2(z * z)
            ms = (_dot(hi, seg_ref[...]) + _dot(lo, seg_ref[...])) * (1.0 / FOX_HEAD_DIM)
            hi, lo = _split2(lax.rsqrt(ms + EPS))
            return z * (_dot(hi, segt_ref[...]) + _dot(lo, segt_ref[...])) * g_ref[...]

        qn = head_norm(qkv[:, :D_MODEL], gq_ref)
        kn = head_norm(qkv[:, D_MODEL:2 * D_MODEL], gk_ref)
        vv = qkv[:, 2 * D_MODEL:]

        x_hi, x_lo = _split2(xn)
        w_hi, w_lo = _split2(wf_ref[...])
        fl = _dot(x_hi, w_hi) + _dot(x_hi, w_lo) + _dot(x_lo, w_hi) + bf_ref[...]
        log_f = jnp.minimum(fl, 0.0) - jnp.log(1.0 + jnp.exp(-jnp.abs(fl)))
        pos16 = t * tm + lax.broadcasted_iota(jnp.int32, log_f.shape, 0)
        log_f = jnp.where(pos16 >= pad, log_f, 0.0)
        tri = tri_ref[...]
        cum = carry_ref[...] + sum(_dot(tri, piece.astype(BF16)) for piece in _split3(log_f))
        carry_ref[...] = cum[tm - 1:tm, :]
        c_hi, c_mid, c_lo = _split3(cum * LOG2E)

        lane = lax.broadcasted_iota(jnp.int32, (tm, LANES), 1)
        pos = t * tm + lax.broadcasted_iota(jnp.int32, (tm, LANES), 0)
        is_head = lane < FOX_HEAD_DIM
        row_k = jnp.where((lane == LANE_KEY_PAD) & (pos < pad), NEG_BIG, rowk_ref[...])
        row_v = jnp.where(lane == LANE_ONE_V, 1.0, 0.0)
        for hd in range(FOX_HEADS):
            col = slice((hd // 2) * LANES, (hd // 2 + 1) * LANES)

            def head(z):
                z = z[:, col]
                return pltpu.roll(z, FOX_HEAD_DIM, axis=1) if hd % 2 else z

            def spread(piece):
                return jnp.broadcast_to(piece[:, hd:hd + 1], (tm, LANES))

            hi, mid, lo = spread(c_hi), spread(c_mid), spread(c_lo)
            aug_q = jnp.where(lane == LANE_C_Q, hi, jnp.where(lane == LANE_C_Q + 1, mid,
                              jnp.where(lane == LANE_C_Q + 2, lo, rowq_ref[...])))
            aug_k = jnp.where(lane == LANE_C_K, -hi, jnp.where(lane == LANE_C_K + 1, -mid,
                              jnp.where(lane == LANE_C_K + 2, -lo, row_k)))
            qa_ref[0, hd] = jnp.where(is_head, head(qn), aug_q).astype(BF16)
            ka_ref[0, hd] = jnp.where(is_head, head(kn), aug_k).astype(BF16)
            va_ref[0, hd] = jnp.where(is_head, head(vv), row_v).astype(BF16)


def _fox_in(h, gain, wqkv, wf, bf, gq, gk, shift, pad):
    bsz, lp, _ = h.shape
    tm = _row_tile(lp)
    nt = lp // tm
    lpa = -(-lp // ATT_BLOCK) * ATT_BLOCK
    head_of = jnp.arange(D_MODEL) // FOX_HEAD_DIM
    seg = (head_of[:, None] == jnp.arange(FOX_HEADS)[None, :]).astype(BF16)
    tri = jnp.tril(jnp.ones((tm, tm), BF16))
    lane = jnp.arange(LANES)
    row_q = jnp.where((lane >= LANE_C_K) & (lane <= LANE_KEY_PAD), 1.0,
                      jnp.where(lane == LANE_SHIFT, -shift * LOG2E, 0.0)).astype(F32)[None, :]
    row_k = (((lane >= LANE_C_Q) & (lane < LANE_C_K)) | (lane == LANE_SHIFT)).astype(F32)[None, :]
    out = pl.BlockSpec((1, FOX_HEADS, tm, LANES), lambda b, t: (b, 0, t, 0))
    return pl.pallas_call(
        functools.partial(_fox_in_kernel, tm=tm, nt=nt, pad=pad),
        grid=(bsz, -(-lpa // tm)),
        in_specs=[pl.BlockSpec((1, tm, D_MODEL), lambda b, t: (b, jnp.minimum(t, nt - 1), 0)),
                  _full((1, D_MODEL)), _full((D_MODEL, 3 * D_MODEL)),
                  _full((D_MODEL, FOX_HEADS)), _full((1, FOX_HEADS)),
                  _full((1, D_MODEL)), _full((1, D_MODEL)),
                  _full((D_MODEL, FOX_HEADS)), _full((FOX_HEADS, D_MODEL)), _full((tm, tm)),
                  _full((1, LANES)), _full((1, LANES))],
        out_specs=[out, out, out],
        out_shape=[jax.ShapeDtypeStruct((bsz, FOX_HEADS, lpa, LANES), BF16)] * 3,
        scratch_shapes=[pltpu.VMEM((1, FOX_HEADS), F32)],
        compiler_params=_params(2),
        name="fox_in",
    )(h, gain, wqkv, wf, bf, gq, gk, seg, seg.T, tri, row_q, row_k)


ATT_Q_STRIPS = 5
ATT_SHIFT_MAX = 40.0


def _flash_kernel(q_ref, k_ref, v_ref, o_ref, acc_ref, *, tq, tk):
    i = pl.program_id(2)
    n_strip = tq // tk
    lane = lax.broadcasted_iota(jnp.int32, (tq, LANES), 1)

    for hd in range(2):
        q = q_ref[0, hd]

        def strip(rows, off, masked):
            s = _dot_nt(q[rows:], k_ref[0, hd, pl.ds(off, tk), :])
            if masked:
                r = lax.broadcasted_iota(jnp.int32, s.shape, 0)
                c = lax.broadcasted_iota(jnp.int32, s.shape, 1)
                s = jnp.where(c <= r, s, NEG_BIG)
            p = jnp.exp2(s).astype(BF16)
            return _dot(p, v_ref[0, hd, pl.ds(off, tk), :])

        def below_diagonal(first, tiles):
            total = None
            for n in range(tiles * n_strip):
                part = strip(0, pl.multiple_of(first * tq + n * tk, tk), False)
                total = part if total is None else total + part
            acc_ref[...] += total

        def tile_pair(j, carry):
            below_diagonal(2 * j, 2)
            return carry

        acc_ref[...] = jnp.zeros_like(acc_ref)
        lax.fori_loop(0, i // 2, tile_pair, 0)

        @pl.when(i % 2 == 1)
        def _():
            below_diagonal(i - 1, 1)

        for n in range(n_strip):
            acc_ref[n * tk:, :] += strip(n * tk, pl.multiple_of(i * tq + n * tk, tk), True)
        acc = acc_ref[...]
        denom = acc[:, FOX_HEAD_DIM:FOX_HEAD_DIM + 1]
        out = acc / jnp.where(denom > 0.0, denom, 1.0)
        if hd == 0:
            o_ref[0] = out.astype(o_ref.dtype)
        else:
            shifted = pltpu.roll(out, FOX_HEAD_DIM, axis=1).astype(o_ref.dtype)
            o_ref[0] = jnp.where(lane < FOX_HEAD_DIM, o_ref[0], shifted)


def _flash_online_kernel(q_ref, k_ref, v_ref, o_ref, *, blk):
    i = pl.program_id(2)
    lane = lax.broadcasted_iota(jnp.int32, (blk, LANES), 1)
    outs = []
    for hd in range(2):
        q = q_ref[0, hd]

        def block(j, m, acc, masked):
            off = pl.multiple_of(j * blk, blk)
            s = _dot_nt(q, k_ref[0, hd, pl.ds(off, blk), :])
            if masked:
                r = lax.broadcasted_iota(jnp.int32, s.shape, 0)
                c = lax.broadcasted_iota(jnp.int32, s.shape, 1)
                s = jnp.where(c <= r, s, NEG_BIG)
            m_new = jnp.maximum(m, jnp.max(s, axis=-1, keepdims=True))
            p = jnp.exp2(s - m_new).astype(BF16)
            return m_new, acc * jnp.exp2(m - m_new) + _dot(p, v_ref[0, hd, pl.ds(off, blk), :])

        m0 = jnp.full((blk, 1), NEG_BIG, F32)
        acc0 = jnp.zeros((blk, LANES), F32)
        m, acc = lax.fori_loop(0, i, lambda j, c: block(j, c[0], c[1], False), (m0, acc0))
        _, acc = block(i, m, acc, True)
        outs.append(acc / acc[:, FOX_HEAD_DIM:FOX_HEAD_DIM + 1])
    o_ref[0] = jnp.where(lane < FOX_HEAD_DIM, outs[0],
                         pltpu.roll(outs[1], FOX_HEAD_DIM, axis=1)).astype(o_ref.dtype)


def _flash(qa, ka, va, lp, online):
    bsz, nh, lpa, _ = qa.shape
    tk = ATT_BLOCK
    tq = tk if online or lpa % (ATT_Q_STRIPS * tk) else ATT_Q_STRIPS * tk
    qspec = pl.BlockSpec((1, 2, tq, LANES), lambda b, h, i: (b, h, i, 0))
    kvspec = pl.BlockSpec((1, 2, lpa, LANES), lambda b, h, i: (b, h, 0, 0))
    if online:
        body, scratch, name = functools.partial(_flash_online_kernel, blk=tq), [], "fox_flash_online"
    else:
        body = functools.partial(_flash_kernel, tq=tq, tk=tk)
        scratch, name = [pltpu.VMEM((tq, LANES), F32)], "fox_flash"
    return pl.pallas_call(
        body,
        grid=(bsz, nh // 2, lpa // tq),
        in_specs=[qspec, kvspec, kvspec],
        out_specs=pl.BlockSpec((1, tq, LANES), lambda b, h, i: (b, i, h)),
        out_shape=jax.ShapeDtypeStruct((bsz, lp, D_MODEL), BF16),
        scratch_shapes=scratch,
        compiler_params=_params(3),
        name=name,
    )(qa, ka, va)


def _out_proj_kernel(a_ref, w_ref, h_ref, o_ref, *, tm, pad):
    t = pl.program_id(1)
    y = h_ref[0] + _dot(a_ref[0], w_ref[...])
    pos = t * tm + lax.broadcasted_iota(jnp.int32, y.shape, 0)
    o_ref[0] = jnp.where(pos >= pad, y, 0.0)


def _out_proj(a, w, h, pad):
    bsz, lp, _ = h.shape
    tm = _row_tile(lp)
    row = pl.BlockSpec((1, tm, D_MODEL), lambda b, t: (b, t, 0))
    return pl.pallas_call(
        functools.partial(_out_proj_kernel, tm=tm, pad=pad),
        grid=(bsz, lp // tm),
        in_specs=[row, _full((D_MODEL, D_MODEL)), row],
        out_specs=row,
        out_shape=jax.ShapeDtypeStruct(h.shape, F32),
        compiler_params=_params(2),
        name="out_proj",
    )(a, w, h)


def _hgrn_kernel(*refs, pad, n_sub, fused_moe):
    if fused_moe:
        mix_refs, refs = refs[:6], refs[6:]
    h_ref, gain_ref, win_ref, bf_ref, lb_ref, go_ref, wout_ref, tri_ref, o_ref, state_ref = refs[:10]
    t = pl.program_id(1)

    @pl.when(t == 0)
    def _():
        state_ref[...] = jnp.zeros_like(state_ref)

    x = h_ref[0]
    if fused_moe:
        d0_ref, d1_ref, d0n_ref, d1n_ref, y_hbm, gate_ref = mix_refs
        buf, sem = refs[10:]
        step = pl.program_id(0) * pl.num_programs(1) + t
        steps = pl.num_programs(0) * pl.num_programs(1)
        x = _moe_mix(step, steps, (d0_ref, d1_ref, d0n_ref, d1n_ref), y_hbm, x, gate_ref, buf, sem,
                     n_sub * HG_CHUNK, prefetch_next=False)
        pos_in = t * (n_sub * HG_CHUNK) + lax.broadcasted_iota(jnp.int32, x.shape, 0)
        x = jnp.where(pos_in >= pad, x, 0.0)

        def prefetch_rows(lo):
            for r in range(lo, lo + HG_SUB):
                _mix_issue(d0n_ref, d1n_ref, y_hbm, buf, sem, 1 - step % 2, r)
    else:
        def prefetch_rows(lo):
            del lo
    proj = _dot(_rms(x, gain_ref[...]).astype(BF16), win_ref[...])
    tri = tri_ref[...]
    row = lax.broadcasted_iota(jnp.int32, (HG_CHUNK, HG_KEY_TOTAL), 0)
    causal = (lax.broadcasted_iota(jnp.int32, (HG_CHUNK, HG_CHUNK), 1)
              <= lax.broadcasted_iota(jnp.int32, (HG_CHUNK, HG_CHUNK), 0))
    heads = [slice(hd * HG_KEY_DIM, (hd + 1) * HG_KEY_DIM) for hd in range(HG_HEADS)]

    q_all = _silu(proj[:, :HG_KEY_TOTAL])
    lb = lb_ref[...]
    f = lb + (1.0 - lb) * _sigmoid(proj[:, HG_KEY_TOTAL:2 * HG_KEY_TOTAL] + bf_ref[...])
    kk_all = 1.0 - f
    v_all = proj[:, 2 * HG_KEY_TOTAL:2 * HG_KEY_TOTAL + D_MODEL]
    lf_hi_all, lf_lo_all = _split2(jnp.log(f))
    states = [state_ref[hd] for hd in range(HG_HEADS)]
    chunks = []
    for c in range(n_sub):
        rs = slice(c * HG_CHUNK, (c + 1) * HG_CHUNK)
        q, kk, v = q_all[rs], kk_all[rs], v_all[rs]
        b = _dot(tri, lf_hi_all[rs]) + _dot(tri, lf_lo_all[rs])
        q_dec = q * jnp.exp(b)
        outs = [_dot_nt(q_dec[:, sl], states[hd]) for hd, sl in enumerate(heads)]
        blocks = [[] for _ in heads]
        for sc in range(HG_CHUNK // HG_SUB):
            lo_r, hi_r = sc * HG_SUB, (sc + 1) * HG_SUB
            ref = b[lo_r - 1:lo_r, :] if sc else jnp.zeros((1, HG_KEY_TOTAL), F32)
            qt = q[lo_r:hi_r] * jnp.exp(b[lo_r:hi_r] - ref)
            kt = kk * jnp.exp(jnp.where(row < hi_r, ref - b, NEG_BIG))
            prefetch_rows(c * HG_CHUNK + lo_r)
            for hd, sl in enumerate(heads):
                blocks[hd].append(_dot_nt(qt[:, sl], kt[:, sl]))
        b_end = b[HG_CHUNK - 1:HG_CHUNK, :]
        k_dec = kk * jnp.exp(b_end - b)
        s_dec = jnp.exp(b_end)
        for hd, sl in enumerate(heads):
            scores = jnp.where(causal, jnp.concatenate(blocks[hd], axis=0), 0.0)
            outs[hd] = _rms(outs[hd] + _dot(scores, v[:, sl]), go_ref[...])
            states[hd] = s_dec[:, sl] * states[hd] + _dot_tn(v[:, sl], k_dec[:, sl])
        chunks.append(jnp.concatenate(outs, axis=-1))
    for hd in range(HG_HEADS):
        state_ref[hd] = states[hd]
    gate = _silu(proj[:, 2 * HG_KEY_TOTAL + D_MODEL:])
    o_all = jnp.concatenate(chunks, axis=0)
    y = x + _dot((o_all * gate).astype(BF16), wout_ref[...])
    pos = t * (n_sub * HG_CHUNK) + lax.broadcasted_iota(jnp.int32, y.shape, 0)
    o_ref[0] = jnp.where(pos >= pad, y, 0.0)
    if fused_moe:
        @pl.when(step == steps - 1)
        def _():
            _mix_wait(y_hbm, buf, sem, 1 - step % 2, n_sub * HG_CHUNK)


def _hgrn_layer(h, gain, win, bf, lb, go, wout, pad, moe=None):
    bsz, lp, _ = h.shape
    hg_in = win.shape[1]
    tm = _row_tile(lp)
    nt = lp // tm
    tri = jnp.tril(jnp.ones((HG_CHUNK, HG_CHUNK), BF16))
    row = pl.BlockSpec((1, tm, D_MODEL), lambda b, t: (b, t, 0))
    operands, specs = (), []
    scratch = [pltpu.VMEM((HG_HEADS, HG_VAL_DIM, HG_KEY_DIM), F32)]
    if moe is not None:
        operands, specs, mix_scratch, _ = _mix_operands(*moe, tm, lambda b, t: b * nt + t)
        scratch += mix_scratch
    return pl.pallas_call(
        functools.partial(_hgrn_kernel, pad=pad, n_sub=tm // HG_CHUNK, fused_moe=moe is not None),
        grid=(bsz, nt),
        in_specs=specs + [row, _full((1, D_MODEL)), _full((D_MODEL, hg_in)), _full((1, HG_KEY_TOTAL)),
                          _full((1, HG_KEY_TOTAL)), _full((1, HG_VAL_DIM)), _full((D_MODEL, D_MODEL)),
                          _full((HG_CHUNK, HG_CHUNK))],
        out_specs=row,
        out_shape=jax.ShapeDtypeStruct(h.shape, F32),
        scratch_shapes=scratch,
        compiler_params=_params(2),
        name="hgrn_layer",
    )(*operands, h, gain, win, bf, lb, go, wout, tri)


def _router_kernel(h_ref, gain_ref, w_ref, b_ref, tri_ref, idx_ref, gate_ref, xn_ref, count_ref):
    @pl.when(pl.program_id(0) == 0)
    def _():
        count_ref[...] = jnp.zeros_like(count_ref)

    xn = _rms(h_ref[...], gain_ref[...])
    xn_ref[...] = xn
    x_hi, x_lo = _split2(xn)
    w_hi, w_lo = _split2(w_ref[...])
    logit = _dot(x_hi, w_hi) + _dot(x_hi, w_lo) + _dot(x_lo, w_hi) + b_ref[...]
    lane = lax.broadcasted_iota(jnp.int32, logit.shape, 1)
    is_grp = (lane >= N_EXPERTS) & (lane < N_EXPERTS + N_GROUPS)
    g_logit = jnp.where(is_grp, logit, -jnp.inf)
    g_max = jnp.max(g_logit, axis=-1, keepdims=True)
    g_idx = jnp.min(jnp.where(g_logit == g_max, lane, 4 * LANES), axis=-1, keepdims=True) - N_EXPERTS
    g_gate = 1.0 / jnp.sum(jnp.exp(g_logit - g_max), axis=-1, keepdims=True)
    in_grp = (lane >= g_idx * EXPERTS_PER_GROUP) & (lane < (g_idx + 1) * EXPERTS_PER_GROUP)
    e1 = jnp.where(in_grp, logit, -jnp.inf)
    v1 = jnp.max(e1, axis=-1, keepdims=True)
    i1 = jnp.min(jnp.where(e1 == v1, lane, 4 * LANES), axis=-1, keepdims=True)
    e2 = jnp.where(lane == i1, -jnp.inf, e1)
    v2 = jnp.max(e2, axis=-1, keepdims=True)
    i2 = jnp.min(jnp.where(e2 == v2, lane, 4 * LANES), axis=-1, keepdims=True)
    z = jnp.exp(v2 - v1)
    w1 = g_gate / (1.0 + z)
    gate_ref[...] = jnp.where(lane == 0, w1, jnp.where(lane == 1, w1 * z, 0.0))
    pick1 = lane == i1
    pick2 = lane == i2
    onehot = jnp.where(pick1 | pick2, 1.0, 0.0)
    before = count_ref[...] + _dot(tri_ref[...], onehot.astype(BF16))
    r1 = jnp.sum(jnp.where(pick1, before, 0.0), axis=-1, keepdims=True).astype(jnp.int32)
    r2 = jnp.sum(jnp.where(pick2, before, 0.0), axis=-1, keepdims=True).astype(jnp.int32)
    tm = onehot.shape[0]
    count_ref[...] = before[tm - 1:tm, :] + onehot[tm - 1:tm, :]
    table = jnp.where(lane == 0, i1, jnp.where(lane == 1, i2,
                      jnp.where(lane == 2, r1, jnp.where(lane == 3, r2, 0)))).astype(F32)
    idx_ref[...] = table.T[:SUBLANES, :].astype(jnp.int32)


def _router(h2, gain, w_cat, b_cat):
    n = h2.shape[0]
    tm = _row_tile(n)
    row = pl.BlockSpec((tm, D_MODEL), lambda t: (t, 0))
    lanes = pl.BlockSpec((tm, LANES), lambda t: (t, 0))
    tri = jnp.tril(jnp.ones((tm, tm), BF16), -1)
    return pl.pallas_call(
        _router_kernel,
        grid=(n // tm,),
        in_specs=[row, _full((1, D_MODEL)), _full((D_MODEL, LANES)), _full((1, LANES)), _full((tm, tm))],
        out_specs=[pl.BlockSpec((SUBLANES, tm), lambda t: (0, t)), lanes, row,
                   _full((1, LANES))],
        out_shape=[jax.ShapeDtypeStruct((SUBLANES, n), jnp.int32),
                   jax.ShapeDtypeStruct((n, LANES), F32),
                   jax.ShapeDtypeStruct((n, D_MODEL), F32),
                   jax.ShapeDtypeStruct((1, LANES), F32)],
        compiler_params=_params(1),
        name="moe_router",
    )(h2, gain, w_cat, b_cat, tri)


def _row_copy(src_ref, dst_ref, sem, src_row, dst_row):
    return pltpu.make_async_copy(src_ref.at[pl.ds(src_row, 1)], dst_ref.at[pl.ds(dst_row, 1)], sem)


def _dispatch_kernel(fill_ref, nb_ref, d0_ref, d1_ref, x_ref, o_hbm, zero_ref, sem, *, tm, n_blocks):
    @pl.when(pl.program_id(0) == 0)
    def _():
        zero_ref[...] = jnp.zeros_like(zero_ref)

        def fill_copy(row):
            return pltpu.make_async_copy(zero_ref, o_hbm.at[pl.ds(row, EXPERT_ROWS)], sem.at[2])

        def tail_start(b, carry):
            fill_copy(pl.multiple_of(b * EXPERT_ROWS, EXPERT_ROWS)).start()
            return carry

        def tail_wait(b, carry):
            fill_copy(pl.multiple_of(b * EXPERT_ROWS, EXPERT_ROWS)).wait()
            return carry

        for e in range(N_EXPERTS):
            fill_copy(pl.multiple_of(fill_ref[e], SUBLANES)).start()
        for e in range(N_EXPERTS):
            fill_copy(pl.multiple_of(fill_ref[e], SUBLANES)).wait()
        lax.fori_loop(nb_ref[0], n_blocks + 1, tail_start, 0)
        lax.fori_loop(nb_ref[0], n_blocks + 1, tail_wait, 0)

    def issue(r, carry):
        _row_copy(x_ref, o_hbm, sem.at[0], r, d0_ref[0, 0, r]).start()
        _row_copy(x_ref, o_hbm, sem.at[1], r, d1_ref[0, 0, r]).start()
        return carry

    lax.fori_loop(0, tm, issue, 0)
    pltpu.make_async_copy(x_ref, o_hbm.at[pl.ds(0, tm)], sem.at[0]).wait()
    pltpu.make_async_copy(x_ref, o_hbm.at[pl.ds(0, tm)], sem.at[1]).wait()


def _dispatch(xn3, dest, fill_start, n_used, n_blocks):
    n = xn3.shape[0]
    tm = 2 * _row_tile(n) if n % (2 * _row_tile(n)) == 0 else _row_tile(n)
    steps = n // tm
    idx = pl.BlockSpec((1, 1, tm), lambda t, fill, nb: (t, 0, 0), memory_space=pltpu.SMEM)
    return pl.pallas_call(
        functools.partial(_dispatch_kernel, tm=tm, n_blocks=n_blocks),
        grid_spec=pltpu.PrefetchScalarGridSpec(
            num_scalar_prefetch=2,
            grid=(steps,),
            in_specs=[idx, idx, pl.BlockSpec((tm, D_MODEL), lambda t, fill, nb: (t, 0))],
            out_specs=pl.BlockSpec(memory_space=pl.ANY),
            scratch_shapes=[pltpu.VMEM((EXPERT_ROWS, D_MODEL), F32),
                            pltpu.SemaphoreType.DMA((3,))]),
        out_shape=jax.ShapeDtypeStruct(((n_blocks + 1) * EXPERT_ROWS, D_MODEL), F32),
        compiler_params=_params(1),
        name="moe_dispatch",
    )(fill_start, n_used, dest[0].reshape(steps, 1, tm), dest[1].reshape(steps, 1, tm), xn3)


def _expert_kernel(be_ref, nb_ref, x_ref, wup_ref, wdn_ref, y_ref):
    del be_ref

    used = pl.program_id(0) < nb_ref[0]

    @pl.when(jnp.logical_not(used))
    def _():
        y_ref[...] = jnp.zeros_like(y_ref)

    @pl.when(used)
    def _():
        gu = _dot(x_ref[...], wup_ref[0])
        act = _silu(gu[:, :D_EXPERT]) * gu[:, D_EXPERT:]
        y_ref[...] = _dot(act, wdn_ref[0])


def _experts(x_pad3, block_expert, n_used, w_up, w_down, n_blocks):
    blk = pl.BlockSpec((EXPERT_ROWS, D_MODEL), lambda i, be, nb: (i, 0))
    return pl.pallas_call(
        _expert_kernel,
        grid_spec=pltpu.PrefetchScalarGridSpec(
            num_scalar_prefetch=2,
            grid=(n_blocks,),
            in_specs=[blk,
                      pl.BlockSpec((1, D_MODEL, 2 * D_EXPERT), lambda i, be, nb: (be[i], 0, 0)),
                      pl.BlockSpec((1, D_EXPERT, D_MODEL), lambda i, be, nb: (be[i], 0, 0))],
            out_specs=blk),
        out_shape=jax.ShapeDtypeStruct((n_blocks * EXPERT_ROWS, D_MODEL), F32),
        compiler_params=_params(1),
        name="moe_experts",
    )(block_expert, n_used, x_pad3, w_up, w_down)


def _mix_issue(i0_ref, i1_ref, y_hbm, buf, sem, into, r):
    _row_copy(y_hbm, buf.at[2 * into], sem.at[2 * into], i0_ref[0, 0, r], r).start()
    _row_copy(y_hbm, buf.at[2 * into + 1], sem.at[2 * into + 1], i1_ref[0, 0, r], r).start()


def _mix_wait(y_hbm, buf, sem, slot, tm):
    for k in range(TOP_K):
        pltpu.make_async_copy(y_hbm.at[pl.ds(0, tm)], buf.at[2 * slot + k], sem.at[2 * slot + k]).wait()


def _moe_mix(step, steps, dest_refs, y_hbm, h, gate_ref, buf, sem, tm, prefetch_next=True):
    d0_ref, d1_ref, d0n_ref, d1n_ref = dest_refs
    slot = step % 2

    def fetch(i0_ref, i1_ref, into):
        def issue(r, carry):
            _mix_issue(i0_ref, i1_ref, y_hbm, buf, sem, into, r)
            return carry

        lax.fori_loop(0, tm, issue, 0)

    @pl.when(step == 0)
    def _():
        fetch(d0_ref, d1_ref, slot)

    if prefetch_next:
        @pl.when(step + 1 < steps)
        def _():
            fetch(d0n_ref, d1n_ref, 1 - slot)

    _mix_wait(y_hbm, buf, sem, slot, tm)
    return h + gate_ref[:, 0:1] * buf[2 * slot] + gate_ref[:, 1:2] * buf[2 * slot + 1]


def _combine_kernel(d0_ref, d1_ref, d0n_ref, d1n_ref, y_hbm, h_ref, gate_ref, o_ref, buf, sem,
                    *, tm, steps, lp, pad):
    t = pl.program_id(0)
    y = _moe_mix(t, steps, (d0_ref, d1_ref, d0n_ref, d1n_ref), y_hbm, h_ref[...], gate_ref, buf, sem, tm)
    pos = (t * tm + lax.broadcasted_iota(jnp.int32, (tm, D_MODEL), 0)) % lp
    o_ref[...] = jnp.where(pos >= pad, y, 0.0)


def _mix_operands(y, dest, gate, tm, step_of):
    steps = dest.shape[1] // tm

    def idx_spec(shift):
        return pl.BlockSpec((1, 1, tm), lambda *g: (jnp.minimum(step_of(*g) + shift, steps - 1), 0, 0),
                            memory_space=pltpu.SMEM)

    d0 = dest[0].reshape(steps, 1, tm)
    d1 = dest[1].reshape(steps, 1, tm)
    specs = [idx_spec(0), idx_spec(0), idx_spec(1), idx_spec(1), pl.BlockSpec(memory_space=pl.ANY),
             pl.BlockSpec((tm, LANES), lambda *g: (step_of(*g), 0))]
    scratch = [pltpu.VMEM((2 * TOP_K, tm, D_MODEL), F32), pltpu.SemaphoreType.DMA((2 * TOP_K,))]
    return (d0, d1, d0, d1, y, gate), specs, scratch, steps


def _combine(h, moe, pad):
    bsz, lp, _ = h.shape
    n = bsz * lp
    tm = _row_tile(lp)
    (d0, d1, d0n, d1n, y, gate), specs, scratch, steps = _mix_operands(*moe, tm, lambda t: t)
    row = pl.BlockSpec((tm, D_MODEL), lambda t: (t, 0))
    return pl.pallas_call(
        functools.partial(_combine_kernel, tm=tm, steps=steps, lp=lp, pad=pad),
        grid=(steps,),
        in_specs=specs[:5] + [row, specs[5]],
        out_specs=row,
        out_shape=jax.ShapeDtypeStruct((n, D_MODEL), F32),
        scratch_shapes=scratch,
        compiler_params=_params(1),
        name="moe_combine",
    )(d0, d1, d0n, d1n, y, h.reshape(n, D_MODEL), gate).reshape(h.shape)


def _moe_layer(h, gain, w_grp, b_grp, w_rt, b_rt, w_up, w_down, layer):
    bsz, lp, _ = h.shape
    n = bsz * lp
    h2 = h.reshape(n, D_MODEL)
    fill = LANES - N_EXPERTS - N_GROUPS
    w_cat = jnp.concatenate([w_rt, w_grp, jnp.zeros((D_MODEL, fill), F32)], axis=1)
    b_cat = jnp.concatenate([b_rt, b_grp, jnp.zeros((fill,), F32)])[None, :]
    idx, gate, xn3, counts = _router(h2, gain, w_cat, b_cat)

    counts = counts[0, :N_EXPERTS].astype(jnp.int32)
    padded = (counts + EXPERT_ROWS - 1) // EXPERT_ROWS * EXPERT_ROWS
    pend = jnp.cumsum(padded)
    pstart = pend - padded
    seg_start = jnp.sum(jnp.where(idx[:TOP_K, :, None] == jnp.arange(N_EXPERTS), pstart, 0), axis=-1)
    dest = seg_start + idx[TOP_K:2 * TOP_K]
    n_blocks = (n * TOP_K + N_EXPERTS * (EXPERT_ROWS - 1) + EXPERT_ROWS - 1) // EXPERT_ROWS
    block_start = jnp.arange(n_blocks, dtype=jnp.int32) * EXPERT_ROWS
    block_expert = jnp.minimum(jnp.sum(block_start[:, None] >= pend[None, :], axis=1),
                               N_EXPERTS - 1).astype(jnp.int32) + layer * N_EXPERTS
    n_used = (pend[N_EXPERTS - 1:] // EXPERT_ROWS).astype(jnp.int32)

    fill_start = (pstart + counts) // SUBLANES * SUBLANES
    x_pad3 = _dispatch(xn3, dest, fill_start, n_used, n_blocks)
    return _experts(x_pad3, block_expert, n_used, w_up, w_down, n_blocks), dest, gate


def _lower_bounds(lb_logits):
    p = jax.nn.softmax(lb_logits.astype(F32), axis=0)
    cum = jnp.cumsum(p, axis=0)
    return cum - cum[:1]


@jax.jit
def kernel(x, meta_tokens, fox_norm, fox_w_in, fox_b_f, fox_q_norm, fox_k_norm, fox_w_out, hg_norm, hg_w_in, hg_b_f, hg_lb_logits, hg_o_norm, hg_w_out, moe_norm, moe_w_grp, moe_b_grp, moe_w_rt, moe_b_rt, moe_w_up, moe_w_down):
    bsz, seq, _ = x.shape
    depth = moe_norm.shape[0]
    length = N_META + seq
    pad = (-length) % ROW_ALIGN
    meta = jnp.broadcast_to(meta_tokens[None].astype(x.dtype), (bsz, N_META, D_MODEL))
    h = jnp.concatenate([jnp.zeros((bsz, pad, D_MODEL), x.dtype), meta, x], axis=1)
    lower_bounds = _lower_bounds(hg_lb_logits)
    scale = FOX_HEAD_DIM ** -0.5
    moe = None
    for i in range(depth):
        j = i // 2
        if i % 2 == 0:
            if moe is not None:
                h = _combine(h, moe, pad)
            shift = FOX_HEAD_DIM * scale * jnp.max(jnp.abs(fox_q_norm[j])) * jnp.max(jnp.abs(fox_k_norm[j]))
            qa, ka, va = _fox_in(
                h, fox_norm[j][None, :], fox_w_in[j][:, :3 * D_MODEL].astype(BF16),
                fox_w_in[j][:, 3 * D_MODEL:], fox_b_f[j][None, :],
                jnp.tile(fox_q_norm[j] * (scale * LOG2E), FOX_HEADS)[None, :],
                jnp.tile(fox_k_norm[j], FOX_HEADS)[None, :], shift, pad)
            o = lax.cond(shift <= ATT_SHIFT_MAX,
                         functools.partial(_flash, lp=h.shape[1], online=False),
                         functools.partial(_flash, lp=h.shape[1], online=True), qa, ka, va)
            h = _out_proj(o, fox_w_out[j].astype(BF16), h, pad)
        else:
            h = _hgrn_layer(h, hg_norm[j][None, :], hg_w_in[j].astype(BF16), hg_b_f[j][None, :],
                            lower_bounds[i][None, :], hg_o_norm[j][None, :],
                            hg_w_out[j].astype(BF16), pad, moe)
        moe = _moe_layer(h, moe_norm[i][None, :], moe_w_grp[i], moe_b_grp[i], moe_w_rt[i], moe_b_rt[i],
                         moe_w_up.reshape(depth * N_EXPERTS, D_MODEL, 2 * D_EXPERT),
                         moe_w_down.reshape(depth * N_EXPERTS, D_EXPERT, D_MODEL), i)
    return _combine(h, moe, pad)[:, pad + N_META:]
```

```python
import functools

import jax
import jax.numpy as jnp
from jax import lax
from jax.experimental import pallas as pl
from jax.experimental.pallas import tpu as pltpu

F32 = jnp.float32
BF16 = jnp.bfloat16

D_MODEL = 1024
N_META = 16
ROW_ALIGN = 128
EPS = 1e-6
NEG_BIG = -1e30

FOX_HEADS = 16
FOX_HEAD_DIM = D_MODEL // FOX_HEADS
HG_HEADS = 8
HG_KEY_DIM = 128
HG_VAL_DIM = D_MODEL // HG_HEADS
HG_KEY_TOTAL = HG_HEADS * HG_KEY_DIM
HG_CHUNK = 128
HG_SUB = 16

N_GROUPS = 4
EXPERTS_PER_GROUP = 8
N_EXPERTS = N_GROUPS * EXPERTS_PER_GROUP
TOP_K = 2
D_EXPERT = D_MODEL // 2
EXPERT_ROWS = 256

LANES = 128
SUBLANES = 8
VMEM_LIMIT = 56 * 1024 * 1024


def _dot(a, b):
    return jnp.dot(a, b, preferred_element_type=F32)


def _dot_nt(a, b):
    return lax.dot_general(a, b, (((1,), (1,)), ((), ())), preferred_element_type=F32)


def _dot_tn(a, b):
    return lax.dot_general(a, b, (((0,), (0,)), ((), ())), preferred_element_type=F32)


def _rms(x, gain):
    return x * lax.rsqrt(jnp.mean(x * x, axis=-1, keepdims=True) + EPS) * gain


def _sigmoid(x):
    return 1.0 / (1.0 + jnp.exp(-x))


def _silu(x):
    return x * _sigmoid(x)


def _row_tile(lp):
    for t in (384, 256, 128):
        if lp % t == 0:
            return t
    raise ValueError(lp)


def _params(n_axes):
    return pltpu.CompilerParams(dimension_semantics=("arbitrary",) * n_axes,
                                vmem_limit_bytes=VMEM_LIMIT)


def _full(shape):
    return pl.BlockSpec(shape, lambda *_: (0,) * len(shape))


ATT_BLOCK = 256
LOG2E = 1.4426950408889634

LANE_C_Q = FOX_HEAD_DIM
LANE_C_K = FOX_HEAD_DIM + 3
LANE_KEY_PAD = FOX_HEAD_DIM + 6
LANE_SHIFT = FOX_HEAD_DIM + 7
LANE_ONE_V = FOX_HEAD_DIM


def _bf16_trunc(x):
    bits = lax.bitcast_convert_type(x, jnp.int32) & jnp.int32(-65536)
    return lax.bitcast_convert_type(bits, F32)


def _split3(c):
    hi = _bf16_trunc(c)
    mid = _bf16_trunc(c - hi)
    return hi, mid, c - hi - mid


def _split2(x):
    hi = x.astype(BF16)
    return hi, (x - hi.astype(F32)).astype(BF16)


def _fox_in_kernel(h_ref, gain_ref, wqkv_ref, wf_ref, bf_ref, gq_ref, gk_ref, seg_ref, segt_ref,
                   tri_ref, rowq_ref, rowk_ref, qa_ref, ka_ref, va_ref, carry_ref, *, tm, nt, pad):
    t = pl.program_id(1)

    @pl.when(t == 0)
    def _():
        carry_ref[...] = jnp.zeros_like(carry_ref)

    @pl.when(t >= nt)
    def _():
        qa_ref[...] = jnp.zeros_like(qa_ref)
        ka_ref[...] = jnp.zeros_like(ka_ref)
        va_ref[...] = jnp.zeros_like(va_ref)

    @pl.when(t < nt)
    def _():
        xn = _rms(h_ref[0], gain_ref[...])
        qkv = _dot(xn.astype(BF16), wqkv_ref[...])

        def head_norm(z, g_ref):
            hi, lo = _split2(z * z)
            ms = (_dot(hi, seg_ref[...]) + _dot(lo, seg_ref[...])) * (1.0 / FOX_HEAD_DIM)
            hi, lo = _split2(lax.rsqrt(ms + EPS))
            return z * (_dot(hi, segt_ref[...]) + _dot(lo, segt_ref[...])) * g_ref[...]

        qn = head_norm(qkv[:, :D_MODEL], gq_ref)
        kn = head_norm(qkv[:, D_MODEL:2 * D_MODEL], gk_ref)
        vv = qkv[:, 2 * D_MODEL:]

        x_hi, x_lo = _split2(xn)
        w_hi, w_lo = _split2(wf_ref[...])
        fl = _dot(x_hi, w_hi) + _dot(x_hi, w_lo) + _dot(x_lo, w_hi) + bf_ref[...]
        log_f = jnp.minimum(fl, 0.0) - jnp.log(1.0 + jnp.exp(-jnp.abs(fl)))
        pos16 = t * tm + lax.broadcasted_iota(jnp.int32, log_f.shape, 0)
        log_f = jnp.where(pos16 >= pad, log_f, 0.0)
        tri = tri_ref[...]
        cum = carry_ref[...] + sum(_dot(tri, piece.astype(BF16)) for piece in _split3(log_f))
        carry_ref[...] = cum[tm - 1:tm, :]
        c_hi, c_mid, c_lo = _split3(cum * LOG2E)

        lane = lax.broadcasted_iota(jnp.int32, (tm, LANES), 1)
        pos = t * tm + lax.broadcasted_iota(jnp.int32, (tm, LANES), 0)
        is_head = lane < FOX_HEAD_DIM
        row_k = jnp.where((lane == LANE_KEY_PAD) & (pos < pad), NEG_BIG, rowk_ref[...])
        row_v = jnp.where(lane == LANE_ONE_V, 1.0, 0.0)
        for hd in range(FOX_HEADS):
            col = slice((hd // 2) * LANES, (hd // 2 + 1) * LANES)

            def head(z):
                z = z[:, col]
                return pltpu.roll(z, FOX_HEAD_DIM, axis=1) if hd % 2 else z

            def spread(piece):
                return jnp.broadcast_to(piece[:, hd:hd + 1], (tm, LANES))

            hi, mid, lo = spread(c_hi), spread(c_mid), spread(c_lo)
            aug_q = jnp.where(lane == LANE_C_Q, hi, jnp.where(lane == LANE_C_Q + 1, mid,
                              jnp.where(lane == LANE_C_Q + 2, lo, rowq_ref[...])))
            aug_k = jnp.where(lane == LANE_C_K, -hi, jnp.where(lane == LANE_C_K + 1, -mid,
                              jnp.where(lane == LANE_C_K + 2, -lo, row_k)))
            qa_ref[0, hd] = jnp.where(is_head, head(qn), aug_q).astype(BF16)
            ka_ref[0, hd] = jnp.where(is_head, head(kn), aug_k).astype(BF16)
            va_ref[0, hd] = jnp.where(is_head, head(vv), row_v).astype(BF16)


def _fox_in(h, gain, wqkv, wf, bf, gq, gk, shift, pad):
    bsz, lp, _ = h.shape
    tm = _row_tile(lp)
    nt = lp // tm
    lpa = -(-lp // ATT_BLOCK) * ATT_BLOCK
    head_of = jnp.arange(D_MODEL) // FOX_HEAD_DIM
    seg = (head_of[:, None] == jnp.arange(FOX_HEADS)[None, :]).astype(BF16)
    tri = jnp.tril(jnp.ones((tm, tm), BF16))
    lane = jnp.arange(LANES)
    row_q = jnp.where((lane >= LANE_C_K) & (lane <= LANE_KEY_PAD), 1.0,
                      jnp.where(lane == LANE_SHIFT, -shift * LOG2E, 0.0)).astype(F32)[None, :]
    row_k = (((lane >= LANE_C_Q) & (lane < LANE_C_K)) | (lane == LANE_SHIFT)).astype(F32)[None, :]
    out = pl.BlockSpec((1, FOX_HEADS, tm, LANES), lambda b, t: (b, 0, t, 0))
    return pl.pallas_call(
        functools.partial(_fox_in_kernel, tm=tm, nt=nt, pad=pad),
        grid=(bsz, -(-lpa // tm)),
        in_specs=[pl.BlockSpec((1, tm, D_MODEL), lambda b, t: (b, jnp.minimum(t, nt - 1), 0)),
                  _full((1, D_MODEL)), _full((D_MODEL, 3 * D_MODEL)),
                  _full((D_MODEL, FOX_HEADS)), _full((1, FOX_HEADS)),
                  _full((1, D_MODEL)), _full((1, D_MODEL)),
                  _full((D_MODEL, FOX_HEADS)), _full((FOX_HEADS, D_MODEL)), _full((tm, tm)),
                  _full((1, LANES)), _full((1, LANES))],
        out_specs=[out, out, out],
        out_shape=[jax.ShapeDtypeStruct((bsz, FOX_HEADS, lpa, LANES), BF16)] * 3,
        scratch_shapes=[pltpu.VMEM((1, FOX_HEADS), F32)],
        compiler_params=_params(2),
        name="fox_in",
    )(h, gain, wqkv, wf, bf, gq, gk, seg, seg.T, tri, row_q, row_k)


ATT_Q_STRIPS = 5
ATT_SHIFT_MAX = 40.0


def _flash_kernel(q_ref, k_ref, v_ref, o_ref, acc_ref, *, tq, tk):
    i = pl.program_id(2)
    n_strip = tq // tk
    lane = lax.broadcasted_iota(jnp.int32, (tq, LANES), 1)

    for hd in range(2):
        q = q_ref[0, hd]

        def strip(rows, off, masked):
            s = _dot_nt(q[rows:], k_ref[0, hd, pl.ds(off, tk), :])
            if masked:
                r = lax.broadcasted_iota(jnp.int32, s.shape, 0)
                c = lax.broadcasted_iota(jnp.int32, s.shape, 1)
                s = jnp.where(c <= r, s, NEG_BIG)
            p = jnp.exp2(s).astype(BF16)
            return _dot(p, v_ref[0, hd, pl.ds(off, tk), :])

        def below_diagonal(first, tiles):
            total = None
            for n in range(tiles * n_strip):
                part = strip(0, pl.multiple_of(first * tq + n * tk, tk), False)
                total = part if total is None else total + part
            acc_ref[...] += total

        def tile_pair(j, carry):
            below_diagonal(2 * j, 2)
            return carry

        acc_ref[...] = jnp.zeros_like(acc_ref)
        lax.fori_loop(0, i // 2, tile_pair, 0)

        @pl.when(i % 2 == 1)
        def _():
            below_diagonal(i - 1, 1)

        for n in range(n_strip):
            acc_ref[n * tk:, :] += strip(n * tk, pl.multiple_of(i * tq + n * tk, tk), True)
        acc = acc_ref[...]
        denom = acc[:, FOX_HEAD_DIM:FOX_HEAD_DIM + 1]
        out = acc / jnp.where(denom > 0.0, denom, 1.0)
        if hd == 0:
            o_ref[0] = out.astype(o_ref.dtype)
        else:
            shifted = pltpu.roll(out, FOX_HEAD_DIM, axis=1).astype(o_ref.dtype)
            o_ref[0] = jnp.where(lane < FOX_HEAD_DIM, o_ref[0], shifted)


def _flash_online_kernel(q_ref, k_ref, v_ref, o_ref, *, blk):
    i = pl.program_id(2)
    lane = lax.broadcasted_iota(jnp.int32, (blk, LANES), 1)
    outs = []
    for hd in range(2):
        q = q_ref[0, hd]

        def block(j, m, acc, masked):
            off = pl.multiple_of(j * blk, blk)
            s = _dot_nt(q, k_ref[0, hd, pl.ds(off, blk), :])
            if masked:
                r = lax.broadcasted_iota(jnp.int32, s.shape, 0)
                c = lax.broadcasted_iota(jnp.int32, s.shape, 1)
                s = jnp.where(c <= r, s, NEG_BIG)
            m_new = jnp.maximum(m, jnp.max(s, axis=-1, keepdims=True))
            p = jnp.exp2(s - m_new).astype(BF16)
            return m_new, acc * jnp.exp2(m - m_new) + _dot(p, v_ref[0, hd, pl.ds(off, blk), :])

        m0 = jnp.full((blk, 1), NEG_BIG, F32)
        acc0 = jnp.zeros((blk, LANES), F32)
        m, acc = lax.fori_loop(0, i, lambda j, c: block(j, c[0], c[1], False), (m0, acc0))
        _, acc = block(i, m, acc, True)
        outs.append(acc / acc[:, FOX_HEAD_DIM:FOX_HEAD_DIM + 1])
    o_ref[0] = jnp.where(lane < FOX_HEAD_DIM, outs[0],
                         pltpu.roll(outs[1], FOX_HEAD_DIM, axis=1)).astype(o_ref.dtype)


def _flash(qa, ka, va, lp, online):
    bsz, nh, lpa, _ = qa.shape
    tk = ATT_BLOCK
    tq = tk if online or lpa % (ATT_Q_STRIPS * tk) else ATT_Q_STRIPS * tk
    qspec = pl.BlockSpec((1, 2, tq, LANES), lambda b, h, i: (b, h, i, 0))
    kvspec = pl.BlockSpec((1, 2, lpa, LANES), lambda b, h, i: (b, h, 0, 0))
    if online:
        body, scratch, name = functools.partial(_flash_online_kernel, blk=tq), [], "fox_flash_online"
    else:
        body = functools.partial(_flash_kernel, tq=tq, tk=tk)
        scratch, name = [pltpu.VMEM((tq, LANES), F32)], "fox_flash"
    return pl.pallas_call(
        body,
        grid=(bsz, nh // 2, lpa // tq),
        in_specs=[qspec, kvspec, kvspec],
        out_specs=pl.BlockSpec((1, tq, LANES), lambda b, h, i: (b, i, h)),
        out_shape=jax.ShapeDtypeStruct((bsz, lp, D_MODEL), BF16),
        scratch_shapes=scratch,
        compiler_params=_params(3),
        name=name,
    )(qa, ka, va)


def _out_proj_kernel(a_ref, w_ref, h_ref, o_ref, *, tm, pad):
    t = pl.program_id(1)
    y = h_ref[0] + _dot(a_ref[0], w_ref[...])
    pos = t * tm + lax.broadcasted_iota(jnp.int32, y.shape, 0)
    o_ref[0] = jnp.where(pos >= pad, y, 0.0)


def _out_proj(a, w, h, pad):
    bsz, lp, _ = h.shape
    tm = _row_tile(lp)
    row = pl.BlockSpec((1, tm, D_MODEL), lambda b, t: (b, t, 0))
    return pl.pallas_call(
        functools.partial(_out_proj_kernel, tm=tm, pad=pad),
        grid=(bsz, lp // tm),
        in_specs=[row, _full((D_MODEL, D_MODEL)), row],
        out_specs=row,
        out_shape=jax.ShapeDtypeStruct(h.shape, F32),
        compiler_params=_params(2),
        name="out_proj",
    )(a, w, h)


def _hgrn_kernel(*refs, pad, n_sub, fused_moe):
    if fused_moe:
        mix_refs, refs = refs[:6], refs[6:]
    h_ref, gain_ref, win_ref, bf_ref, lb_ref, go_ref, wout_ref, tri_ref, o_ref, state_ref = refs[:10]
    t = pl.program_id(1)

    @pl.when(t == 0)
    def _():
        state_ref[...] = jnp.zeros_like(state_ref)

    x = h_ref[0]
    if fused_moe:
        d0_ref, d1_ref, d0n_ref, d1n_ref, y_hbm, gate_ref = mix_refs
        buf, sem = refs[10:]
        step = pl.program_id(0) * pl.num_programs(1) + t
        steps = pl.num_programs(0) * pl.num_programs(1)
        x = _moe_mix(step, steps, (d0_ref, d1_ref, d0n_ref, d1n_ref), y_hbm, x, gate_ref, buf, sem,
                     n_sub * HG_CHUNK, prefetch_next=False)
        pos_in = t * (n_sub * HG_CHUNK) + lax.broadcasted_iota(jnp.int32, x.shape, 0)
        x = jnp.where(pos_in >= pad, x, 0.0)

        def prefetch_rows(lo):
            for r in range(lo, lo + HG_SUB):
                _mix_issue(d0n_ref, d1n_ref, y_hbm, buf, sem, 1 - step % 2, r)
    else:
        def prefetch_rows(lo):
            del lo
    proj = _dot(_rms(x, gain_ref[...]).astype(BF16), win_ref[...])
    tri = tri_ref[...]
    row = lax.broadcasted_iota(jnp.int32, (HG_CHUNK, HG_KEY_TOTAL), 0)
    causal = (lax.broadcasted_iota(jnp.int32, (HG_CHUNK, HG_CHUNK), 1)
              <= lax.broadcasted_iota(jnp.int32, (HG_CHUNK, HG_CHUNK), 0))
    heads = [slice(hd * HG_KEY_DIM, (hd + 1) * HG_KEY_DIM) for hd in range(HG_HEADS)]

    q_all = _silu(proj[:, :HG_KEY_TOTAL])
    lb = lb_ref[...]
    f = lb + (1.0 - lb) * _sigmoid(proj[:, HG_KEY_TOTAL:2 * HG_KEY_TOTAL] + bf_ref[...])
    kk_all = 1.0 - f
    v_all = proj[:, 2 * HG_KEY_TOTAL:2 * HG_KEY_TOTAL + D_MODEL]
    lf_hi_all, lf_lo_all = _split2(jnp.log(f))
    states = [state_ref[hd] for hd in range(HG_HEADS)]
    chunks = []
    for c in range(n_sub):
        rs = slice(c * HG_CHUNK, (c + 1) * HG_CHUNK)
        q, kk, v = q_all[rs], kk_all[rs], v_all[rs]
        b = _dot(tri, lf_hi_all[rs]) + _dot(tri, lf_lo_all[rs])
        q_dec = q * jnp.exp(b)
        outs = [_dot_nt(q_dec[:, sl], states[hd]) for hd, sl in enumerate(heads)]
        blocks = [[] for _ in heads]
        for sc in range(HG_CHUNK // HG_SUB):
            lo_r, hi_r = sc * HG_SUB, (sc + 1) * HG_SUB
            ref = b[lo_r - 1:lo_r, :] if sc else jnp.zeros((1, HG_KEY_TOTAL), F32)
            qt = q[lo_r:hi_r] * jnp.exp(b[lo_r:hi_r] - ref)
            kt = kk * jnp.exp(jnp.where(row < hi_r, ref - b, NEG_BIG))
            prefetch_rows(c * HG_CHUNK + lo_r)
            for hd, sl in enumerate(heads):
                blocks[hd].append(_dot_nt(qt[:, sl], kt[:, sl]))
        b_end = b[HG_CHUNK - 1:HG_CHUNK, :]
        k_dec = kk * jnp.exp(b_end - b)
        s_dec = jnp.exp(b_end)
        for hd, sl in enumerate(heads):
            scores = jnp.where(causal, jnp.concatenate(blocks[hd], axis=0), 0.0)
            outs[hd] = _rms(outs[hd] + _dot(scores, v[:, sl]), go_ref[...])
            states[hd] = s_dec[:, sl] * states[hd] + _dot_tn(v[:, sl], k_dec[:, sl])
        chunks.append(jnp.concatenate(outs, axis=-1))
    for hd in range(HG_HEADS):
        state_ref[hd] = states[hd]
    gate = _silu(proj[:, 2 * HG_KEY_TOTAL + D_MODEL:])
    o_all = jnp.concatenate(chunks, axis=0)
    y = x + _dot((o_all * gate).astype(BF16), wout_ref[...])
    pos = t * (n_sub * HG_CHUNK) + lax.broadcasted_iota(jnp.int32, y.shape, 0)
    o_ref[0] = jnp.where(pos >= pad, y, 0.0)
    if fused_moe:
        @pl.when(step == steps - 1)
        def _():
            _mix_wait(y_hbm, buf, sem, 1 - step % 2, n_sub * HG_CHUNK)


def _hgrn_layer(h, gain, win, bf, lb, go, wout, pad, moe=None):
    bsz, lp, _ = h.shape
    hg_in = win.shape[1]
    tm = _row_tile(lp)
    nt = lp // tm
    tri = jnp.tril(jnp.ones((HG_CHUNK, HG_CHUNK), BF16))
    row = pl.BlockSpec((1, tm, D_MODEL), lambda b, t: (b, t, 0))
    operands, specs = (), []
    scratch = [pltpu.VMEM((HG_HEADS, HG_VAL_DIM, HG_KEY_DIM), F32)]
    if moe is not None:
        operands, specs, mix_scratch, _ = _mix_operands(*moe, tm, lambda b, t: b * nt + t)
        scratch += mix_scratch
    return pl.pallas_call(
        functools.partial(_hgrn_kernel, pad=pad, n_sub=tm // HG_CHUNK, fused_moe=moe is not None),
        grid=(bsz, nt),
        in_specs=specs + [row, _full((1, D_MODEL)), _full((D_MODEL, hg_in)), _full((1, HG_KEY_TOTAL)),
                          _full((1, HG_KEY_TOTAL)), _full((1, HG_VAL_DIM)), _full((D_MODEL, D_MODEL)),
                          _full((HG_CHUNK, HG_CHUNK))],
        out_specs=row,
        out_shape=jax.ShapeDtypeStruct(h.shape, F32),
        scratch_shapes=scratch,
        compiler_params=_params(2),
        name="hgrn_layer",
    )(*operands, h, gain, win, bf, lb, go, wout, tri)


def _router_kernel(h_ref, gain_ref, w_ref, b_ref, tri_ref, idx_ref, gate_ref, xn_ref, count_ref):
    @pl.when(pl.program_id(0) == 0)
    def _():
        count_ref[...] = jnp.zeros_like(count_ref)

    xn = _rms(h_ref[...], gain_ref[...])
    xn_ref[...] = xn
    x_hi, x_lo = _split2(xn)
    w_hi, w_lo = _split2(w_ref[...])
    logit = _dot(x_hi, w_hi) + _dot(x_hi, w_lo) + _dot(x_lo, w_hi) + b_ref[...]
    lane = lax.broadcasted_iota(jnp.int32, logit.shape, 1)

    def first_lane_of(hit):
        return jnp.min(jnp.where(hit, lane, LANES), axis=-1, keepdims=True)

    is_grp = (lane >= N_EXPERTS) & (lane < N_EXPERTS + N_GROUPS)
    g_logit = jnp.where(is_grp, logit, -jnp.inf)
    g_max = jnp.max(g_logit, axis=-1, keepdims=True)
    g_idx = first_lane_of(g_logit == g_max) - N_EXPERTS
    g_gate = 1.0 / jnp.sum(jnp.exp(g_logit - g_max), axis=-1, keepdims=True)
    in_grp = (lane >= g_idx * EXPERTS_PER_GROUP) & (lane < (g_idx + 1) * EXPERTS_PER_GROUP)
    e1 = jnp.where(in_grp, logit, -jnp.inf)
    v1 = jnp.max(e1, axis=-1, keepdims=True)
    i1 = first_lane_of(e1 == v1)
    e2 = jnp.where(lane == i1, -jnp.inf, e1)
    v2 = jnp.max(e2, axis=-1, keepdims=True)
    i2 = first_lane_of(e2 == v2)
    z = jnp.exp(v2 - v1)
    w1 = g_gate / (1.0 + z)
    gate_ref[...] = jnp.where(lane == 0, w1, jnp.where(lane == 1, w1 * z, 0.0))
    pick1 = lane == i1
    pick2 = lane == i2
    onehot = jnp.where(pick1 | pick2, 1.0, 0.0)
    before = count_ref[...] + _dot(tri_ref[...], onehot.astype(BF16))
    r1 = jnp.sum(jnp.where(pick1, before, 0.0), axis=-1, keepdims=True).astype(jnp.int32)
    r2 = jnp.sum(jnp.where(pick2, before, 0.0), axis=-1, keepdims=True).astype(jnp.int32)
    tm = onehot.shape[0]
    count_ref[...] = before[tm - 1:tm, :] + onehot[tm - 1:tm, :]
    table = jnp.where(lane == 0, i1, jnp.where(lane == 1, i2,
                      jnp.where(lane == 2, r1, jnp.where(lane == 3, r2, 0)))).astype(F32)
    idx_ref[...] = table.T[:SUBLANES, :].astype(jnp.int32)


def _router(h2, gain, w_cat, b_cat):
    n = h2.shape[0]
    tm = _row_tile(n)
    row = pl.BlockSpec((tm, D_MODEL), lambda t: (t, 0))
    lanes = pl.BlockSpec((tm, LANES), lambda t: (t, 0))
    tri = jnp.tril(jnp.ones((tm, tm), BF16), -1)
    return pl.pallas_call(
        _router_kernel,
        grid=(n // tm,),
        in_specs=[row, _full((1, D_MODEL)), _full((D_MODEL, LANES)), _full((1, LANES)), _full((tm, tm))],
        out_specs=[pl.BlockSpec((SUBLANES, tm), lambda t: (0, t)), lanes, row,
                   _full((1, LANES))],
        out_shape=[jax.ShapeDtypeStruct((SUBLANES, n), jnp.int32),
                   jax.ShapeDtypeStruct((n, LANES), F32),
                   jax.ShapeDtypeStruct((n, D_MODEL), F32),
                   jax.ShapeDtypeStruct((1, LANES), F32)],
        compiler_params=_params(1),
        name="moe_router",
    )(h2, gain, w_cat, b_cat, tri)


def _row_copy(src_ref, dst_ref, sem, src_row, dst_row):
    return pltpu.make_async_copy(src_ref.at[pl.ds(src_row, 1)], dst_ref.at[pl.ds(dst_row, 1)], sem)


def _dispatch_kernel(fill_ref, nb_ref, d0_ref, d1_ref, x_ref, o_hbm, zero_ref, sem, *, tm, n_blocks):
    @pl.when(pl.program_id(0) == 0)
    def _():
        zero_ref[...] = jnp.zeros_like(zero_ref)

        def fill_copy(row):
            return pltpu.make_async_copy(zero_ref, o_hbm.at[pl.ds(row, EXPERT_ROWS)], sem.at[2])

        def tail_start(b, carry):
            fill_copy(pl.multiple_of(b * EXPERT_ROWS, EXPERT_ROWS)).start()
            return carry

        def tail_wait(b, carry):
            fill_copy(pl.multiple_of(b * EXPERT_ROWS, EXPERT_ROWS)).wait()
            return carry

        for e in range(N_EXPERTS):
            fill_copy(pl.multiple_of(fill_ref[e], SUBLANES)).start()
        for e in range(N_EXPERTS):
            fill_copy(pl.multiple_of(fill_ref[e], SUBLANES)).wait()
        lax.fori_loop(nb_ref[0], n_blocks + 1, tail_start, 0)
        lax.fori_loop(nb_ref[0], n_blocks + 1, tail_wait, 0)

    def issue(r, carry):
        _row_copy(x_ref, o_hbm, sem.at[0], r, d0_ref[0, 0, r]).start()
        _row_copy(x_ref, o_hbm, sem.at[1], r, d1_ref[0, 0, r]).start()
        return carry

    lax.fori_loop(0, tm, issue, 0)
    pltpu.make_async_copy(x_ref, o_hbm.at[pl.ds(0, tm)], sem.at[0]).wait()
    pltpu.make_async_copy(x_ref, o_hbm.at[pl.ds(0, tm)], sem.at[1]).wait()


def _dispatch(xn3, dest, fill_start, n_used, n_blocks):
    n = xn3.shape[0]
    tm = 2 * _row_tile(n) if n % (2 * _row_tile(n)) == 0 else _row_tile(n)
    steps = n // tm
    idx = pl.BlockSpec((1, 1, tm), lambda t, fill, nb: (t, 0, 0), memory_space=pltpu.SMEM)
    return pl.pallas_call(
        functools.partial(_dispatch_kernel, tm=tm, n_blocks=n_blocks),
        grid_spec=pltpu.PrefetchScalarGridSpec(
            num_scalar_prefetch=2,
            grid=(steps,),
            in_specs=[idx, idx, pl.BlockSpec((tm, D_MODEL), lambda t, fill, nb: (t, 0))],
            out_specs=pl.BlockSpec(memory_space=pl.ANY),
            scratch_shapes=[pltpu.VMEM((EXPERT_ROWS, D_MODEL), F32),
                            pltpu.SemaphoreType.DMA((3,))]),
        out_shape=jax.ShapeDtypeStruct(((n_blocks + 1) * EXPERT_ROWS, D_MODEL), F32),
        compiler_params=_params(1),
        name="moe_dispatch",
    )(fill_start, n_used, dest[0].reshape(steps, 1, tm), dest[1].reshape(steps, 1, tm), xn3)


def _expert_kernel(be_ref, nb_ref, x_ref, wup_ref, wdn_ref, y_ref):
    del be_ref

    used = pl.program_id(0) < nb_ref[0]

    @pl.when(jnp.logical_not(used))
    def _():
        y_ref[...] = jnp.zeros_like(y_ref)

    @pl.when(used)
    def _():
        gu = _dot(x_ref[...], wup_ref[0])
        act = _silu(gu[:, :D_EXPERT]) * gu[:, D_EXPERT:]
        y_ref[...] = _dot(act, wdn_ref[0])


def _experts(x_pad3, block_expert, n_used, w_up, w_down, n_blocks):
    blk = pl.BlockSpec((EXPERT_ROWS, D_MODEL), lambda i, be, nb: (i, 0))
    return pl.pallas_call(
        _expert_kernel,
        grid_spec=pltpu.PrefetchScalarGridSpec(
            num_scalar_prefetch=2,
            grid=(n_blocks,),
            in_specs=[blk,
                      pl.BlockSpec((1, D_MODEL, 2 * D_EXPERT), lambda i, be, nb: (be[i], 0, 0)),
                      pl.BlockSpec((1, D_EXPERT, D_MODEL), lambda i, be, nb: (be[i], 0, 0))],
            out_specs=blk),
        out_shape=jax.ShapeDtypeStruct((n_blocks * EXPERT_ROWS, D_MODEL), F32),
        compiler_params=_params(1),
        name="moe_experts",
    )(block_expert, n_used, x_pad3, w_up, w_down)


def _mix_issue(i0_ref, i1_ref, y_hbm, buf, sem, into, r):
    _row_copy(y_hbm, buf.at[2 * into], sem.at[2 * into], i0_ref[0, 0, r], r).start()
    _row_copy(y_hbm, buf.at[2 * into + 1], sem.at[2 * into + 1], i1_ref[0, 0, r], r).start()


def _mix_wait(y_hbm, buf, sem, slot, tm):
    for k in range(TOP_K):
        pltpu.make_async_copy(y_hbm.at[pl.ds(0, tm)], buf.at[2 * slot + k], sem.at[2 * slot + k]).wait()


def _moe_mix(step, steps, dest_refs, y_hbm, h, gate_ref, buf, sem, tm, prefetch_next=True):
    d0_ref, d1_ref, d0n_ref, d1n_ref = dest_refs
    slot = step % 2

    def fetch(i0_ref, i1_ref, into):
        def issue(r, carry):
            _mix_issue(i0_ref, i1_ref, y_hbm, buf, sem, into, r)
            return carry

        lax.fori_loop(0, tm, issue, 0)

    @pl.when(step == 0)
    def _():
        fetch(d0_ref, d1_ref, slot)

    if prefetch_next:
        @pl.when(step + 1 < steps)
        def _():
            fetch(d0n_ref, d1n_ref, 1 - slot)

    _mix_wait(y_hbm, buf, sem, slot, tm)
    return h + gate_ref[:, 0:1] * buf[2 * slot] + gate_ref[:, 1:2] * buf[2 * slot + 1]


def _combine_kernel(d0_ref, d1_ref, d0n_ref, d1n_ref, y_hbm, h_ref, gate_ref, o_ref, buf, sem,
                    *, tm, steps, lp, pad):
    t = pl.program_id(0)
    y = _moe_mix(t, steps, (d0_ref, d1_ref, d0n_ref, d1n_ref), y_hbm, h_ref[...], gate_ref, buf, sem, tm)
    pos = (t * tm + lax.broadcasted_iota(jnp.int32, (tm, D_MODEL), 0)) % lp
    o_ref[...] = jnp.where(pos >= pad, y, 0.0)


def _mix_operands(y, dest, gate, tm, step_of):
    steps = dest.shape[1] // tm

    def idx_spec(shift):
        return pl.BlockSpec((1, 1, tm), lambda *g: (jnp.minimum(step_of(*g) + shift, steps - 1), 0, 0),
                            memory_space=pltpu.SMEM)

    d0 = dest[0].reshape(steps, 1, tm)
    d1 = dest[1].reshape(steps, 1, tm)
    specs = [idx_spec(0), idx_spec(0), idx_spec(1), idx_spec(1), pl.BlockSpec(memory_space=pl.ANY),
             pl.BlockSpec((tm, LANES), lambda *g: (step_of(*g), 0))]
    scratch = [pltpu.VMEM((2 * TOP_K, tm, D_MODEL), F32), pltpu.SemaphoreType.DMA((2 * TOP_K,))]
    return (d0, d1, d0, d1, y, gate), specs, scratch, steps


def _combine(h, moe, pad):
    bsz, lp, _ = h.shape
    n = bsz * lp
    tm = _row_tile(lp)
    (d0, d1, d0n, d1n, y, gate), specs, scratch, steps = _mix_operands(*moe, tm, lambda t: t)
    row = pl.BlockSpec((tm, D_MODEL), lambda t: (t, 0))
    return pl.pallas_call(
        functools.partial(_combine_kernel, tm=tm, steps=steps, lp=lp, pad=pad),
        grid=(steps,),
        in_specs=specs[:5] + [row, specs[5]],
        out_specs=row,
        out_shape=jax.ShapeDtypeStruct((n, D_MODEL), F32),
        scratch_shapes=scratch,
        compiler_params=_params(1),
        name="moe_combine",
    )(d0, d1, d0n, d1n, y, h.reshape(n, D_MODEL), gate).reshape(h.shape)


def _moe_layer(h, gain, w_grp, b_grp, w_rt, b_rt, w_up, w_down, layer):
    bsz, lp, _ = h.shape
    n = bsz * lp
    h2 = h.reshape(n, D_MODEL)
    fill = LANES - N_EXPERTS - N_GROUPS
    w_cat = jnp.concatenate([w_rt, w_grp, jnp.zeros((D_MODEL, fill), F32)], axis=1)
    b_cat = jnp.concatenate([b_rt, b_grp, jnp.zeros((fill,), F32)])[None, :]
    idx, gate, xn3, counts = _router(h2, gain, w_cat, b_cat)

    counts = counts[0, :N_EXPERTS].astype(jnp.int32)
    padded = (counts + EXPERT_ROWS - 1) // EXPERT_ROWS * EXPERT_ROWS
    pend = jnp.cumsum(padded)
    pstart = pend - padded
    seg_start = jnp.sum(jnp.where(idx[:TOP_K, :, None] == jnp.arange(N_EXPERTS), pstart, 0), axis=-1)
    dest = seg_start + idx[TOP_K:2 * TOP_K]
    n_blocks = (n * TOP_K + N_EXPERTS * (EXPERT_ROWS - 1) + EXPERT_ROWS - 1) // EXPERT_ROWS
    block_start = jnp.arange(n_blocks, dtype=jnp.int32) * EXPERT_ROWS
    block_expert = jnp.minimum(jnp.sum(block_start[:, None] >= pend[None, :], axis=1),
                               N_EXPERTS - 1).astype(jnp.int32) + layer * N_EXPERTS
    n_used = (pend[N_EXPERTS - 1:] // EXPERT_ROWS).astype(jnp.int32)

    fill_start = (pstart + counts) // SUBLANES * SUBLANES
    x_pad3 = _dispatch(xn3, dest, fill_start, n_used, n_blocks)
    return _experts(x_pad3, block_expert, n_used, w_up, w_down, n_blocks), dest, gate


def _lower_bounds(lb_logits):
    p = jax.nn.softmax(lb_logits.astype(F32), axis=0)
    cum = jnp.cumsum(p, axis=0)
    return cum - cum[:1]


@jax.jit
def kernel(x, meta_tokens, fox_norm, fox_w_in, fox_b_f, fox_q_norm, fox_k_norm, fox_w_out, hg_norm, hg_w_in, hg_b_f, hg_lb_logits, hg_o_norm, hg_w_out, moe_norm, moe_w_grp, moe_b_grp, moe_w_rt, moe_b_rt, moe_w_up, moe_w_down):
    bsz, seq, _ = x.shape
    depth = moe_norm.shape[0]
    length = N_META + seq
    pad = (-length) % ROW_ALIGN
    meta = jnp.broadcast_to(meta_tokens[None].astype(x.dtype), (bsz, N_META, D_MODEL))
    h = jnp.concatenate([jnp.zeros((bsz, pad, D_MODEL), x.dtype), meta, x], axis=1)
    lower_bounds = _lower_bounds(hg_lb_logits)
    scale = FOX_HEAD_DIM ** -0.5
    moe = None
    for i in range(depth):
        j = i // 2
        if i % 2 == 0:
            if moe is not None:
                h = _combine(h, moe, pad)
            shift = FOX_HEAD_DIM * scale * jnp.max(jnp.abs(fox_q_norm[j])) * jnp.max(jnp.abs(fox_k_norm[j]))
            qa, ka, va = _fox_in(
                h, fox_norm[j][None, :], fox_w_in[j][:, :3 * D_MODEL].astype(BF16),
                fox_w_in[j][:, 3 * D_MODEL:], fox_b_f[j][None, :],
                jnp.tile(fox_q_norm[j] * (scale * LOG2E), FOX_HEADS)[None, :],
                jnp.tile(fox_k_norm[j], FOX_HEADS)[None, :], shift, pad)
            o = lax.cond(shift <= ATT_SHIFT_MAX,
                         functools.partial(_flash, lp=h.shape[1], online=False),
                         functools.partial(_flash, lp=h.shape[1], online=True), qa, ka, va)
            h = _out_proj(o, fox_w_out[j].astype(BF16), h, pad)
        else:
            h = _hgrn_layer(h, hg_norm[j][None, :], hg_w_in[j].astype(BF16), hg_b_f[j][None, :],
                            lower_bounds[i][None, :], hg_o_norm[j][None, :],
                            hg_w_out[j].astype(BF16), pad, moe)
        moe = _moe_layer(h, moe_norm[i][None, :], moe_w_grp[i], moe_b_grp[i], moe_w_rt[i], moe_b_rt[i],
                         moe_w_up.reshape(depth * N_EXPERTS, D_MODEL, 2 * D_EXPERT),
                         moe_w_down.reshape(depth * N_EXPERTS, D_EXPERT, D_MODEL), i)
    return _combine(h, moe, pad)[:, pad + N_META:]
```

```python
import functools

import jax
import jax.numpy as jnp
from jax import lax
from jax.experimental import pallas as pl
from jax.experimental.pallas import tpu as pltpu

F32 = jnp.float32
BF16 = jnp.bfloat16

D_MODEL = 1024
N_META = 16
ROW_ALIGN = 128
EPS = 1e-6
NEG_BIG = -1e30

FOX_HEADS = 16
FOX_HEAD_DIM = D_MODEL // FOX_HEADS
HG_HEADS = 8
HG_KEY_DIM = 128
HG_VAL_DIM = D_MODEL // HG_HEADS
HG_KEY_TOTAL = HG_HEADS * HG_KEY_DIM
HG_CHUNK = 128
HG_SUB = 16

N_GROUPS = 4
EXPERTS_PER_GROUP = 8
N_EXPERTS = N_GROUPS * EXPERTS_PER_GROUP
TOP_K = 2
D_EXPERT = D_MODEL // 2
EXPERT_ROWS = 512

LANES = 128
SUBLANES = 8
VMEM_LIMIT = 56 * 1024 * 1024


def _dot(a, b):
    return jnp.dot(a, b, preferred_element_type=F32)


def _dot_nt(a, b):
    return lax.dot_general(a, b, (((1,), (1,)), ((), ())), preferred_element_type=F32)


def _dot_tn(a, b):
    return lax.dot_general(a, b, (((0,), (0,)), ((), ())), preferred_element_type=F32)


def _rms(x, gain):
    return x * lax.rsqrt(jnp.mean(x * x, axis=-1, keepdims=True) + EPS) * gain


def _sigmoid(x):
    return 1.0 / (1.0 + jnp.exp(-x))


def _silu(x):
    return x * _sigmoid(x)


def _row_tile(lp):
    for t in (384, 256, 128):
        if lp % t == 0:
            return t
    raise ValueError(lp)


def _params(n_axes):
    return pltpu.CompilerParams(dimension_semantics=("arbitrary",) * n_axes,
                                vmem_limit_bytes=VMEM_LIMIT)


def _full(shape):
    return pl.BlockSpec(shape, lambda *_: (0,) * len(shape))


ATT_BLOCK = 256
LOG2E = 1.4426950408889634

LANE_C_Q = FOX_HEAD_DIM
LANE_C_K = FOX_HEAD_DIM + 3
LANE_KEY_PAD = FOX_HEAD_DIM + 6
LANE_SHIFT = FOX_HEAD_DIM + 7
LANE_ONE_V = FOX_HEAD_DIM


def _bf16_trunc(x):
    bits = lax.bitcast_convert_type(x, jnp.int32) & jnp.int32(-65536)
    return lax.bitcast_convert_type(bits, F32)


def _split3(c):
    hi = _bf16_trunc(c)
    mid = _bf16_trunc(c - hi)
    return hi, mid, c - hi - mid


def _split2(x):
    hi = x.astype(BF16)
    return hi, (x - hi.astype(F32)).astype(BF16)


def _fox_in_kernel(h_ref, gain_ref, wqkv_ref, wf_ref, bf_ref, gq_ref, gk_ref, seg_ref, segt_ref,
                   tri_ref, rowq_ref, rowk_ref, qa_ref, ka_ref, va_ref, carry_ref, *, tm, nt, pad):
    t = pl.program_id(1)

    @pl.when(t == 0)
    def _():
        carry_ref[...] = jnp.zeros_like(carry_ref)

    @pl.when(t >= nt)
    def _():
        qa_ref[...] = jnp.zeros_like(qa_ref)
        ka_ref[...] = jnp.zeros_like(ka_ref)
        va_ref[...] = jnp.zeros_like(va_ref)

    @pl.when(t < nt)
    def _():
        xn = _rms(h_ref[0], gain_ref[...])
        qkv = _dot(xn.astype(BF16), wqkv_ref[...])

        def head_norm(z, g_ref):
            hi, lo = _split2(z * z)
            ms = (_dot(hi, seg_ref[...]) + _dot(lo, seg_ref[...])) * (1.0 / FOX_HEAD_DIM)
            hi, lo = _split2(lax.rsqrt(ms + EPS))
            return z * (_dot(hi, segt_ref[...]) + _dot(lo, segt_ref[...])) * g_ref[...]

        qn = head_norm(qkv[:, :D_MODEL], gq_ref)
        kn = head_norm(qkv[:, D_MODEL:2 * D_MODEL], gk_ref)
        vv = qkv[:, 2 * D_MODEL:]

        x_hi, x_lo = _split2(xn)
        w_hi, w_lo = _split2(wf_ref[...])
        fl = _dot(x_hi, w_hi) + _dot(x_hi, w_lo) + _dot(x_lo, w_hi) + bf_ref[...]
        log_f = jnp.minimum(fl, 0.0) - jnp.log(1.0 + jnp.exp(-jnp.abs(fl)))
        pos16 = t * tm + lax.broadcasted_iota(jnp.int32, log_f.shape, 0)
        log_f = jnp.where(pos16 >= pad, log_f, 0.0)
        tri = tri_ref[...]
        cum = carry_ref[...] + sum(_dot(tri, piece.astype(BF16)) for piece in _split3(log_f))
        carry_ref[...] = cum[tm - 1:tm, :]
        c_hi, c_mid, c_lo = _split3(cum * LOG2E)

        lane = lax.broadcasted_iota(jnp.int32, (tm, LANES), 1)
        pos = t * tm + lax.broadcasted_iota(jnp.int32, (tm, LANES), 0)
        is_head = lane < FOX_HEAD_DIM
        row_k = jnp.where((lane == LANE_KEY_PAD) & (pos < pad), NEG_BIG, rowk_ref[...])
        row_v = jnp.where(lane == LANE_ONE_V, 1.0, 0.0)
        for hd in range(FOX_HEADS):
            col = slice((hd // 2) * LANES, (hd // 2 + 1) * LANES)

            def head(z):
                z = z[:, col]
                return pltpu.roll(z, FOX_HEAD_DIM, axis=1) if hd % 2 else z

            def spread(piece):
                return jnp.broadcast_to(piece[:, hd:hd + 1], (tm, LANES))

            hi, mid, lo = spread(c_hi), spread(c_mid), spread(c_lo)
            aug_q = jnp.where(lane == LANE_C_Q, hi, jnp.where(lane == LANE_C_Q + 1, mid,
                              jnp.where(lane == LANE_C_Q + 2, lo, rowq_ref[...])))
            aug_k = jnp.where(lane == LANE_C_K, -hi, jnp.where(lane == LANE_C_K + 1, -mid,
                              jnp.where(lane == LANE_C_K + 2, -lo, row_k)))
            qa_ref[0, hd] = jnp.where(is_head, head(qn), aug_q).astype(BF16)
            ka_ref[0, hd] = jnp.where(is_head, head(kn), aug_k).astype(BF16)
            va_ref[0, hd] = jnp.where(is_head, head(vv), row_v).astype(BF16)


def _fox_in(h, gain, wqkv, wf, bf, gq, gk, shift, pad):
    bsz, lp, _ = h.shape
    tm = _row_tile(lp)
    nt = lp // tm
    lpa = -(-lp // ATT_BLOCK) * ATT_BLOCK
    head_of = jnp.arange(D_MODEL) // FOX_HEAD_DIM
    seg = (head_of[:, None] == jnp.arange(FOX_HEADS)[None, :]).astype(BF16)
    tri = jnp.tril(jnp.ones((tm, tm), BF16))
    lane = jnp.arange(LANES)
    row_q = jnp.where((lane >= LANE_C_K) & (lane <= LANE_KEY_PAD), 1.0,
                      jnp.where(lane == LANE_SHIFT, -shift * LOG2E, 0.0)).astype(F32)[None, :]
    row_k = (((lane >= LANE_C_Q) & (lane < LANE_C_K)) | (lane == LANE_SHIFT)).astype(F32)[None, :]
    out = pl.BlockSpec((1, FOX_HEADS, tm, LANES), lambda b, t: (b, 0, t, 0))
    return pl.pallas_call(
        functools.partial(_fox_in_kernel, tm=tm, nt=nt, pad=pad),
        grid=(bsz, -(-lpa // tm)),
        in_specs=[pl.BlockSpec((1, tm, D_MODEL), lambda b, t: (b, jnp.minimum(t, nt - 1), 0)),
                  _full((1, D_MODEL)), _full((D_MODEL, 3 * D_MODEL)),
                  _full((D_MODEL, FOX_HEADS)), _full((1, FOX_HEADS)),
                  _full((1, D_MODEL)), _full((1, D_MODEL)),
                  _full((D_MODEL, FOX_HEADS)), _full((FOX_HEADS, D_MODEL)), _full((tm, tm)),
                  _full((1, LANES)), _full((1, LANES))],
        out_specs=[out, out, out],
        out_shape=[jax.ShapeDtypeStruct((bsz, FOX_HEADS, lpa, LANES), BF16)] * 3,
        scratch_shapes=[pltpu.VMEM((1, FOX_HEADS), F32)],
        compiler_params=_params(2),
        name="fox_in",
    )(h, gain, wqkv, wf, bf, gq, gk, seg, seg.T, tri, row_q, row_k)


ATT_Q_STRIPS = 5
ATT_SHIFT_MAX = 40.0


def _flash_kernel(q_ref, k_ref, v_ref, o_ref, acc_ref, *, tq, tk):
    i = pl.program_id(2)
    n_strip = tq // tk
    lane = lax.broadcasted_iota(jnp.int32, (tq, LANES), 1)

    for hd in range(2):
        q = q_ref[0, hd]

        def strip(rows, off, masked):
            s = _dot_nt(q[rows:], k_ref[0, hd, pl.ds(off, tk), :])
            if masked:
                r = lax.broadcasted_iota(jnp.int32, s.shape, 0)
                c = lax.broadcasted_iota(jnp.int32, s.shape, 1)
                s = jnp.where(c <= r, s, NEG_BIG)
            p = jnp.exp2(s).astype(BF16)
            return _dot(p, v_ref[0, hd, pl.ds(off, tk), :])

        def below_diagonal(first, tiles):
            total = None
            for n in range(tiles * n_strip):
                part = strip(0, pl.multiple_of(first * tq + n * tk, tk), False)
                total = part if total is None else total + part
            acc_ref[...] += total

        def tile_pair(j, carry):
            below_diagonal(2 * j, 2)
            return carry

        acc_ref[...] = jnp.zeros_like(acc_ref)
        lax.fori_loop(0, i // 2, tile_pair, 0)

        @pl.when(i % 2 == 1)
        def _():
            below_diagonal(i - 1, 1)

        for n in range(n_strip):
            acc_ref[n * tk:, :] += strip(n * tk, pl.multiple_of(i * tq + n * tk, tk), True)
        acc = acc_ref[...]
        denom = acc[:, FOX_HEAD_DIM:FOX_HEAD_DIM + 1]
        out = acc / jnp.where(denom > 0.0, denom, 1.0)
        if hd == 0:
            o_ref[0] = out.astype(o_ref.dtype)
        else:
            shifted = pltpu.roll(out, FOX_HEAD_DIM, axis=1).astype(o_ref.dtype)
            o_ref[0] = jnp.where(lane < FOX_HEAD_DIM, o_ref[0], shifted)


def _flash_online_kernel(q_ref, k_ref, v_ref, o_ref, *, blk):
    i = pl.program_id(2)
    lane = lax.broadcasted_iota(jnp.int32, (blk, LANES), 1)
    outs = []
    for hd in range(2):
        q = q_ref[0, hd]

        def block(j, m, acc, masked):
            off = pl.multiple_of(j * blk, blk)
            s = _dot_nt(q, k_ref[0, hd, pl.ds(off, blk), :])
            if masked:
                r = lax.broadcasted_iota(jnp.int32, s.shape, 0)
                c = lax.broadcasted_iota(jnp.int32, s.shape, 1)
                s = jnp.where(c <= r, s, NEG_BIG)
            m_new = jnp.maximum(m, jnp.max(s, axis=-1, keepdims=True))
            p = jnp.exp2(s - m_new).astype(BF16)
            return m_new, acc * jnp.exp2(m - m_new) + _dot(p, v_ref[0, hd, pl.ds(off, blk), :])

        m0 = jnp.full((blk, 1), NEG_BIG, F32)
        acc0 = jnp.zeros((blk, LANES), F32)
        m, acc = lax.fori_loop(0, i, lambda j, c: block(j, c[0], c[1], False), (m0, acc0))
        _, acc = block(i, m, acc, True)
        outs.append(acc / acc[:, FOX_HEAD_DIM:FOX_HEAD_DIM + 1])
    o_ref[0] = jnp.where(lane < FOX_HEAD_DIM, outs[0],
                         pltpu.roll(outs[1], FOX_HEAD_DIM, axis=1)).astype(o_ref.dtype)


def _flash(qa, ka, va, lp, online):
    bsz, nh, lpa, _ = qa.shape
    tk = ATT_BLOCK
    tq = tk if online or lpa % (ATT_Q_STRIPS * tk) else ATT_Q_STRIPS * tk
    qspec = pl.BlockSpec((1, 2, tq, LANES), lambda b, h, i: (b, h, i, 0))
    kvspec = pl.BlockSpec((1, 2, lpa, LANES), lambda b, h, i: (b, h, 0, 0))
    if online:
        body, scratch, name = functools.partial(_flash_online_kernel, blk=tq), [], "fox_flash_online"
    else:
        body = functools.partial(_flash_kernel, tq=tq, tk=tk)
        scratch, name = [pltpu.VMEM((tq, LANES), F32)], "fox_flash"
    return pl.pallas_call(
        body,
        grid=(bsz, nh // 2, lpa // tq),
        in_specs=[qspec, kvspec, kvspec],
        out_specs=pl.BlockSpec((1, tq, LANES), lambda b, h, i: (b, i, h)),
        out_shape=jax.ShapeDtypeStruct((bsz, lp, D_MODEL), BF16),
        scratch_shapes=scratch,
        compiler_params=_params(3),
        name=name,
    )(qa, ka, va)


def _out_proj_kernel(a_ref, w_ref, h_ref, o_ref, *, tm, pad):
    t = pl.program_id(1)
    y = h_ref[0] + _dot(a_ref[0], w_ref[...])
    pos = t * tm + lax.broadcasted_iota(jnp.int32, y.shape, 0)
    o_ref[0] = jnp.where(pos >= pad, y, 0.0)


def _out_proj(a, w, h, pad):
    bsz, lp, _ = h.shape
    tm = _row_tile(lp)
    row = pl.BlockSpec((1, tm, D_MODEL), lambda b, t: (b, t, 0))
    return pl.pallas_call(
        functools.partial(_out_proj_kernel, tm=tm, pad=pad),
        grid=(bsz, lp // tm),
        in_specs=[row, _full((D_MODEL, D_MODEL)), row],
        out_specs=row,
        out_shape=jax.ShapeDtypeStruct(h.shape, F32),
        compiler_params=_params(2),
        name="out_proj",
    )(a, w, h)


def _hgrn_kernel(*refs, pad, n_sub, fused_moe):
    if fused_moe:
        mix_refs, refs = refs[:6], refs[6:]
    h_ref, gain_ref, win_ref, bf_ref, lb_ref, go_ref, wout_ref, tri_ref, o_ref, state_ref = refs[:10]
    t = pl.program_id(1)

    @pl.when(t == 0)
    def _():
        state_ref[...] = jnp.zeros_like(state_ref)

    x = h_ref[0]
    if fused_moe:
        d0_ref, d1_ref, d0n_ref, d1n_ref, y_hbm, gate_ref = mix_refs
        buf, sem = refs[10:]
        step = pl.program_id(0) * pl.num_programs(1) + t
        steps = pl.num_programs(0) * pl.num_programs(1)
        x = _moe_mix(step, steps, (d0_ref, d1_ref, d0n_ref, d1n_ref), y_hbm, x, gate_ref, buf, sem,
                     n_sub * HG_CHUNK, prefetch_next=False)
        pos_in = t * (n_sub * HG_CHUNK) + lax.broadcasted_iota(jnp.int32, x.shape, 0)
        x = jnp.where(pos_in >= pad, x, 0.0)

        def prefetch_rows(lo):
            for r in range(lo, lo + HG_SUB):
                _mix_issue(d0n_ref, d1n_ref, y_hbm, buf, sem, 1 - step % 2, r)
    else:
        def prefetch_rows(lo):
            del lo
    proj = _dot(_rms(x, gain_ref[...]).astype(BF16), win_ref[...])
    tri = tri_ref[...]
    row = lax.broadcasted_iota(jnp.int32, (HG_CHUNK, HG_KEY_TOTAL), 0)
    causal = (lax.broadcasted_iota(jnp.int32, (HG_CHUNK, HG_CHUNK), 1)
              <= lax.broadcasted_iota(jnp.int32, (HG_CHUNK, HG_CHUNK), 0))
    heads = [slice(hd * HG_KEY_DIM, (hd + 1) * HG_KEY_DIM) for hd in range(HG_HEADS)]

    q_all = _silu(proj[:, :HG_KEY_TOTAL])
    lb = lb_ref[...]
    f = lb + (1.0 - lb) * _sigmoid(proj[:, HG_KEY_TOTAL:2 * HG_KEY_TOTAL] + bf_ref[...])
    kk_all = 1.0 - f
    v_all = proj[:, 2 * HG_KEY_TOTAL:2 * HG_KEY_TOTAL + D_MODEL]
    lf_hi_all, lf_lo_all = _split2(jnp.log(f))
    states = [state_ref[hd] for hd in range(HG_HEADS)]
    chunks = []
    for c in range(n_sub):
        rs = slice(c * HG_CHUNK, (c + 1) * HG_CHUNK)
        q, kk, v = q_all[rs], kk_all[rs], v_all[rs]
        b = _dot(tri, lf_hi_all[rs]) + _dot(tri, lf_lo_all[rs])
        q_dec = q * jnp.exp(b)
        outs = [_dot_nt(q_dec[:, sl], states[hd]) for hd, sl in enumerate(heads)]
        blocks = [[] for _ in heads]
        for sc in range(HG_CHUNK // HG_SUB):
            lo_r, hi_r = sc * HG_SUB, (sc + 1) * HG_SUB
            ref = b[lo_r - 1:lo_r, :] if sc else jnp.zeros((1, HG_KEY_TOTAL), F32)
            qt = q[lo_r:hi_r] * jnp.exp(b[lo_r:hi_r] - ref)
            kt = kk * jnp.exp(jnp.where(row < hi_r, ref - b, NEG_BIG))
            prefetch_rows(c * HG_CHUNK + lo_r)
            for hd, sl in enumerate(heads):
                blocks[hd].append(_dot_nt(qt[:, sl], kt[:, sl]))
        b_end = b[HG_CHUNK - 1:HG_CHUNK, :]
        k_dec = kk * jnp.exp(b_end - b)
        s_dec = jnp.exp(b_end)
        for hd, sl in enumerate(heads):
            scores = jnp.where(causal, jnp.concatenate(blocks[hd], axis=0), 0.0)
            outs[hd] = _rms(outs[hd] + _dot(scores, v[:, sl]), go_ref[...])
            states[hd] = s_dec[:, sl] * states[hd] + _dot_tn(v[:, sl], k_dec[:, sl])
        chunks.append(jnp.concatenate(outs, axis=-1))
    for hd in range(HG_HEADS):
        state_ref[hd] = states[hd]
    gate = _silu(proj[:, 2 * HG_KEY_TOTAL + D_MODEL:])
    o_all = jnp.concatenate(chunks, axis=0)
    y = x + _dot((o_all * gate).astype(BF16), wout_ref[...])
    pos = t * (n_sub * HG_CHUNK) + lax.broadcasted_iota(jnp.int32, y.shape, 0)
    o_ref[0] = jnp.where(pos >= pad, y, 0.0)
    if fused_moe:
        @pl.when(step == steps - 1)
        def _():
            _mix_wait(y_hbm, buf, sem, 1 - step % 2, n_sub * HG_CHUNK)


def _hgrn_layer(h, gain, win, bf, lb, go, wout, pad, moe=None):
    bsz, lp, _ = h.shape
    hg_in = win.shape[1]
    tm = _row_tile(lp)
    nt = lp // tm
    tri = jnp.tril(jnp.ones((HG_CHUNK, HG_CHUNK), BF16))
    row = pl.BlockSpec((1, tm, D_MODEL), lambda b, t: (b, t, 0))
    operands, specs = (), []
    scratch = [pltpu.VMEM((HG_HEADS, HG_VAL_DIM, HG_KEY_DIM), F32)]
    if moe is not None:
        operands, specs, mix_scratch, _ = _mix_operands(*moe, tm, lambda b, t: b * nt + t)
        scratch += mix_scratch
    return pl.pallas_call(
        functools.partial(_hgrn_kernel, pad=pad, n_sub=tm // HG_CHUNK, fused_moe=moe is not None),
        grid=(bsz, nt),
        in_specs=specs + [row, _full((1, D_MODEL)), _full((D_MODEL, hg_in)), _full((1, HG_KEY_TOTAL)),
                          _full((1, HG_KEY_TOTAL)), _full((1, HG_VAL_DIM)), _full((D_MODEL, D_MODEL)),
                          _full((HG_CHUNK, HG_CHUNK))],
        out_specs=row,
        out_shape=jax.ShapeDtypeStruct(h.shape, F32),
        scratch_shapes=scratch,
        compiler_params=_params(2),
        name="hgrn_layer",
    )(*operands, h, gain, win, bf, lb, go, wout, tri)


def _router_kernel(h_ref, gain_ref, w_ref, b_ref, tri_ref, idx_ref, gate_ref, xn_ref, count_ref):
    @pl.when(pl.program_id(0) == 0)
    def _():
        count_ref[...] = jnp.zeros_like(count_ref)

    xn = _rms(h_ref[...], gain_ref[...])
    xn_ref[...] = xn
    x_hi, x_lo = _split2(xn)
    w_hi, w_lo = _split2(w_ref[...])
    logit = _dot(x_hi, w_hi) + _dot(x_hi, w_lo) + _dot(x_lo, w_hi) + b_ref[...]
    lane = lax.broadcasted_iota(jnp.int32, logit.shape, 1)

    def first_lane_of(hit):
        return jnp.min(jnp.where(hit, lane, LANES), axis=-1, keepdims=True)

    is_grp = (lane >= N_EXPERTS) & (lane < N_EXPERTS + N_GROUPS)
    g_logit = jnp.where(is_grp, logit, -jnp.inf)
    g_max = jnp.max(g_logit, axis=-1, keepdims=True)
    g_idx = first_lane_of(g_logit == g_max) - N_EXPERTS
    g_gate = 1.0 / jnp.sum(jnp.exp(g_logit - g_max), axis=-1, keepdims=True)
    in_grp = (lane >= g_idx * EXPERTS_PER_GROUP) & (lane < (g_idx + 1) * EXPERTS_PER_GROUP)
    e1 = jnp.where(in_grp, logit, -jnp.inf)
    v1 = jnp.max(e1, axis=-1, keepdims=True)
    i1 = first_lane_of(e1 == v1)
    e2 = jnp.where(lane == i1, -jnp.inf, e1)
    v2 = jnp.max(e2, axis=-1, keepdims=True)
    i2 = first_lane_of(e2 == v2)
    z = jnp.exp(v2 - v1)
    w1 = g_gate / (1.0 + z)
    gate_ref[...] = jnp.where(lane == 0, w1, jnp.where(lane == 1, w1 * z, 0.0))
    pick1 = lane == i1
    pick2 = lane == i2
    onehot = jnp.where(pick1 | pick2, 1.0, 0.0)
    before = count_ref[...] + _dot(tri_ref[...], onehot.astype(BF16))
    r1 = jnp.sum(jnp.where(pick1, before, 0.0), axis=-1, keepdims=True).astype(jnp.int32)
    r2 = jnp.sum(jnp.where(pick2, before, 0.0), axis=-1, keepdims=True).astype(jnp.int32)
    tm = onehot.shape[0]
    count_ref[...] = before[tm - 1:tm, :] + onehot[tm - 1:tm, :]
    table = jnp.where(lane == 0, i1, jnp.where(lane == 1, i2,
                      jnp.where(lane == 2, r1, jnp.where(lane == 3, r2, 0)))).astype(F32)
    idx_ref[...] = table.T[:SUBLANES, :].astype(jnp.int32)


def _router(h2, gain, w_cat, b_cat):
    n = h2.shape[0]
    tm = _row_tile(n)
    row = pl.BlockSpec((tm, D_MODEL), lambda t: (t, 0))
    lanes = pl.BlockSpec((tm, LANES), lambda t: (t, 0))
    tri = jnp.tril(jnp.ones((tm, tm), BF16), -1)
    return pl.pallas_call(
        _router_kernel,
        grid=(n // tm,),
        in_specs=[row, _full((1, D_MODEL)), _full((D_MODEL, LANES)), _full((1, LANES)), _full((tm, tm))],
        out_specs=[pl.BlockSpec((SUBLANES, tm), lambda t: (0, t)), lanes, row,
                   _full((1, LANES))],
        out_shape=[jax.ShapeDtypeStruct((SUBLANES, n), jnp.int32),
                   jax.ShapeDtypeStruct((n, LANES), F32),
                   jax.ShapeDtypeStruct((n, D_MODEL), F32),
                   jax.ShapeDtypeStruct((1, LANES), F32)],
        compiler_params=_params(1),
        name="moe_router",
    )(h2, gain, w_cat, b_cat, tri)


def _row_copy(src_ref, dst_ref, sem, src_row, dst_row):
    return pltpu.make_async_copy(src_ref.at[pl.ds(src_row, 1)], dst_ref.at[pl.ds(dst_row, 1)], sem)


def _dispatch_kernel(fill_ref, nb_ref, d0_ref, d1_ref, x_ref, o_hbm, zero_ref, sem, *, tm, n_blocks):
    @pl.when(pl.program_id(0) == 0)
    def _():
        zero_ref[...] = jnp.zeros_like(zero_ref)

        def fill_copy(row):
            return pltpu.make_async_copy(zero_ref, o_hbm.at[pl.ds(row, EXPERT_ROWS)], sem.at[2])

        def tail_start(b, carry):
            fill_copy(pl.multiple_of(b * EXPERT_ROWS, EXPERT_ROWS)).start()
            return carry

        def tail_wait(b, carry):
            fill_copy(pl.multiple_of(b * EXPERT_ROWS, EXPERT_ROWS)).wait()
            return carry

        for e in range(N_EXPERTS):
            fill_copy(pl.multiple_of(fill_ref[e], SUBLANES)).start()
        for e in range(N_EXPERTS):
            fill_copy(pl.multiple_of(fill_ref[e], SUBLANES)).wait()
        lax.fori_loop(nb_ref[0], n_blocks + 1, tail_start, 0)
        lax.fori_loop(nb_ref[0], n_blocks + 1, tail_wait, 0)

    def issue(r, carry):
        _row_copy(x_ref, o_hbm, sem.at[0], r, d0_ref[0, 0, r]).start()
        _row_copy(x_ref, o_hbm, sem.at[1], r, d1_ref[0, 0, r]).start()
        return carry

    lax.fori_loop(0, tm, issue, 0)
    pltpu.make_async_copy(x_ref, o_hbm.at[pl.ds(0, tm)], sem.at[0]).wait()
    pltpu.make_async_copy(x_ref, o_hbm.at[pl.ds(0, tm)], sem.at[1]).wait()


def _dispatch(xn3, dest, fill_start, n_used, n_blocks):
    n = xn3.shape[0]
    tm = 2 * _row_tile(n) if n % (2 * _row_tile(n)) == 0 else _row_tile(n)
    steps = n // tm
    idx = pl.BlockSpec((1, 1, tm), lambda t, fill, nb: (t, 0, 0), memory_space=pltpu.SMEM)
    return pl.pallas_call(
        functools.partial(_dispatch_kernel, tm=tm, n_blocks=n_blocks),
        grid_spec=pltpu.PrefetchScalarGridSpec(
            num_scalar_prefetch=2,
            grid=(steps,),
            in_specs=[idx, idx, pl.BlockSpec((tm, D_MODEL), lambda t, fill, nb: (t, 0))],
            out_specs=pl.BlockSpec(memory_space=pl.ANY),
            scratch_shapes=[pltpu.VMEM((EXPERT_ROWS, D_MODEL), F32),
                            pltpu.SemaphoreType.DMA((3,))]),
        out_shape=jax.ShapeDtypeStruct(((n_blocks + 1) * EXPERT_ROWS, D_MODEL), F32),
        compiler_params=_params(1),
        name="moe_dispatch",
    )(fill_start, n_used, dest[0].reshape(steps, 1, tm), dest[1].reshape(steps, 1, tm), xn3)


def _expert_kernel(be_ref, nb_ref, x_ref, wup_ref, wdn_ref, y_ref):
    del be_ref

    used = pl.program_id(0) < nb_ref[0]

    @pl.when(jnp.logical_not(used))
    def _():
        y_ref[...] = jnp.zeros_like(y_ref)

    @pl.when(used)
    def _():
        gu = _dot(x_ref[...], wup_ref[0])
        act = _silu(gu[:, :D_EXPERT]) * gu[:, D_EXPERT:]
        y_ref[...] = _dot(act, wdn_ref[0])


def _experts(x_pad3, block_expert, n_used, w_up, w_down, n_blocks):
    blk = pl.BlockSpec((EXPERT_ROWS, D_MODEL), lambda i, be, nb: (i, 0))
    return pl.pallas_call(
        _expert_kernel,
        grid_spec=pltpu.PrefetchScalarGridSpec(
            num_scalar_prefetch=2,
            grid=(n_blocks,),
            in_specs=[blk,
                      pl.BlockSpec((1, D_MODEL, 2 * D_EXPERT), lambda i, be, nb: (be[i], 0, 0)),
                      pl.BlockSpec((1, D_EXPERT, D_MODEL), lambda i, be, nb: (be[i], 0, 0))],
            out_specs=blk),
        out_shape=jax.ShapeDtypeStruct((n_blocks * EXPERT_ROWS, D_MODEL), F32),
        compiler_params=_params(1),
        name="moe_experts",
    )(block_expert, n_used, x_pad3, w_up, w_down)


def _mix_issue(i0_ref, i1_ref, y_hbm, buf, sem, into, r):
    _row_copy(y_hbm, buf.at[2 * into], sem.at[2 * into], i0_ref[0, 0, r], r).start()
    _row_copy(y_hbm, buf.at[2 * into + 1], sem.at[2 * into + 1], i1_ref[0, 0, r], r).start()


def _mix_wait(y_hbm, buf, sem, slot, tm):
    for k in range(TOP_K):
        pltpu.make_async_copy(y_hbm.at[pl.ds(0, tm)], buf.at[2 * slot + k], sem.at[2 * slot + k]).wait()


def _moe_mix(step, steps, dest_refs, y_hbm, h, gate_ref, buf, sem, tm, prefetch_next=True):
    d0_ref, d1_ref, d0n_ref, d1n_ref = dest_refs
    slot = step % 2

    def fetch(i0_ref, i1_ref, into):
        def issue(r, carry):
            _mix_issue(i0_ref, i1_ref, y_hbm, buf, sem, into, r)
            return carry

        lax.fori_loop(0, tm, issue, 0)

    @pl.when(step == 0)
    def _():
        fetch(d0_ref, d1_ref, slot)

    if prefetch_next:
        @pl.when(step + 1 < steps)
        def _():
            fetch(d0n_ref, d1n_ref, 1 - slot)

    _mix_wait(y_hbm, buf, sem, slot, tm)
    return h + gate_ref[:, 0:1] * buf[2 * slot] + gate_ref[:, 1:2] * buf[2 * slot + 1]


def _combine_kernel(d0_ref, d1_ref, d0n_ref, d1n_ref, y_hbm, h_ref, gate_ref, o_ref, buf, sem,
                    *, tm, steps, lp, pad):
    t = pl.program_id(0)
    y = _moe_mix(t, steps, (d0_ref, d1_ref, d0n_ref, d1n_ref), y_hbm, h_ref[...], gate_ref, buf, sem, tm)
    pos = (t * tm + lax.broadcasted_iota(jnp.int32, (tm, D_MODEL), 0)) % lp
    o_ref[...] = jnp.where(pos >= pad, y, 0.0)


def _mix_operands(y, dest, gate, tm, step_of):
    steps = dest.shape[1] // tm

    def idx_spec(shift):
        return pl.BlockSpec((1, 1, tm), lambda *g: (jnp.minimum(step_of(*g) + shift, steps - 1), 0, 0),
                            memory_space=pltpu.SMEM)

    d0 = dest[0].reshape(steps, 1, tm)
    d1 = dest[1].reshape(steps, 1, tm)
    specs = [idx_spec(0), idx_spec(0), idx_spec(1), idx_spec(1), pl.BlockSpec(memory_space=pl.ANY),
             pl.BlockSpec((tm, LANES), lambda *g: (step_of(*g), 0))]
    scratch = [pltpu.VMEM((2 * TOP_K, tm, D_MODEL), F32), pltpu.SemaphoreType.DMA((2 * TOP_K,))]
    return (d0, d1, d0, d1, y, gate), specs, scratch, steps


def _combine(h, moe, pad):
    bsz, lp, _ = h.shape
    n = bsz * lp
    tm = _row_tile(lp)
    (d0, d1, d0n, d1n, y, gate), specs, scratch, steps = _mix_operands(*moe, tm, lambda t: t)
    row = pl.BlockSpec((tm, D_MODEL), lambda t: (t, 0))
    return pl.pallas_call(
        functools.partial(_combine_kernel, tm=tm, steps=steps, lp=lp, pad=pad),
        grid=(steps,),
        in_specs=specs[:5] + [row, specs[5]],
        out_specs=row,
        out_shape=jax.ShapeDtypeStruct((n, D_MODEL), F32),
        scratch_shapes=scratch,
        compiler_params=_params(1),
        name="moe_combine",
    )(d0, d1, d0n, d1n, y, h.reshape(n, D_MODEL), gate).reshape(h.shape)


def _moe_layer(h, gain, w_grp, b_grp, w_rt, b_rt, w_up, w_down, layer):
    bsz, lp, _ = h.shape
    n = bsz * lp
    h2 = h.reshape(n, D_MODEL)
    fill = LANES - N_EXPERTS - N_GROUPS
    w_cat = jnp.concatenate([w_rt, w_grp, jnp.zeros((D_MODEL, fill), F32)], axis=1)
    b_cat = jnp.concatenate([b_rt, b_grp, jnp.zeros((fill,), F32)])[None, :]
    idx, gate, xn3, counts = _router(h2, gain, w_cat, b_cat)

    counts = counts[0, :N_EXPERTS].astype(jnp.int32)
    padded = (counts + EXPERT_ROWS - 1) // EXPERT_ROWS * EXPERT_ROWS
    pend = jnp.cumsum(padded)
    pstart = pend - padded
    seg_start = jnp.sum(jnp.where(idx[:TOP_K, :, None] == jnp.arange(N_EXPERTS), pstart, 0), axis=-1)
    dest = seg_start + idx[TOP_K:2 * TOP_K]
    n_blocks = (n * TOP_K + N_EXPERTS * (EXPERT_ROWS - 1) + EXPERT_ROWS - 1) // EXPERT_ROWS
    block_start = jnp.arange(n_blocks, dtype=jnp.int32) * EXPERT_ROWS
    block_expert = jnp.minimum(jnp.sum(block_start[:, None] >= pend[None, :], axis=1),
                               N_EXPERTS - 1).astype(jnp.int32) + layer * N_EXPERTS
    n_used = (pend[N_EXPERTS - 1:] // EXPERT_ROWS).astype(jnp.int32)

    fill_start = (pstart + counts) // SUBLANES * SUBLANES
    x_pad3 = _dispatch(xn3, dest, fill_start, n_used, n_blocks)
    return _experts(x_pad3, block_expert, n_used, w_up, w_down, n_blocks), dest, gate


def _lower_bounds(lb_logits):
    p = jax.nn.softmax(lb_logits.astype(F32), axis=0)
    cum = jnp.cumsum(p, axis=0)
    return cum - cum[:1]


@jax.jit
def kernel(x, meta_tokens, fox_norm, fox_w_in, fox_b_f, fox_q_norm, fox_k_norm, fox_w_out, hg_norm, hg_w_in, hg_b_f, hg_lb_logits, hg_o_norm, hg_w_out, moe_norm, moe_w_grp, moe_b_grp, moe_w_rt, moe_b_rt, moe_w_up, moe_w_down):
    bsz, seq, _ = x.shape
    depth = moe_norm.shape[0]
    length = N_META + seq
    pad = (-length) % ROW_ALIGN
    meta = jnp.broadcast_to(meta_tokens[None].astype(x.dtype), (bsz, N_META, D_MODEL))
    h = jnp.concatenate([jnp.zeros((bsz, pad, D_MODEL), x.dtype), meta, x], axis=1)
    lower_bounds = _lower_bounds(hg_lb_logits)
    scale = FOX_HEAD_DIM ** -0.5
    moe = None
    for i in range(depth):
        j = i // 2
        if i % 2 == 0:
            if moe is not None:
                h = _combine(h, moe, pad)
            shift = FOX_HEAD_DIM * scale * jnp.max(jnp.abs(fox_q_norm[j])) * jnp.max(jnp.abs(fox_k_norm[j]))
            qa, ka, va = _fox_in(
                h, fox_norm[j][None, :], fox_w_in[j][:, :3 * D_MODEL].astype(BF16),
                fox_w_in[j][:, 3 * D_MODEL:], fox_b_f[j][None, :],
                jnp.tile(fox_q_norm[j] * (scale * LOG2E), FOX_HEADS)[None, :],
                jnp.tile(fox_k_norm[j], FOX_HEADS)[None, :], shift, pad)
            o = lax.cond(shift <= ATT_SHIFT_MAX,
                         functools.partial(_flash, lp=h.shape[1], online=False),
                         functools.partial(_flash, lp=h.shape[1], online=True), qa, ka, va)
            h = _out_proj(o, fox_w_out[j].astype(BF16), h, pad)
        else:
            h = _hgrn_layer(h, hg_norm[j][None, :], hg_w_in[j].astype(BF16), hg_b_f[j][None, :],
                            lower_bounds[i][None, :], hg_o_norm[j][None, :],
                            hg_w_out[j].astype(BF16), pad, moe)
        moe = _moe_layer(h, moe_norm[i][None, :], moe_w_grp[i], moe_b_grp[i], moe_w_rt[i], moe_b_rt[i],
                         moe_w_up.reshape(depth * N_EXPERTS, D_MODEL, 2 * D_EXPERT),
                         moe_w_down.reshape(depth * N_EXPERTS, D_EXPERT, D_MODEL), i)
    return _combine(h, moe, pad)[:, pad + N_META:]
```

```python
import functools

import jax
import jax.numpy as jnp
from jax import lax
from jax.experimental import pallas as pl
from jax.experimental.pallas import tpu as pltpu

F32 = jnp.float32
BF16 = jnp.bfloat16

D_MODEL = 1024
N_META = 16
ROW_ALIGN = 128
EPS = 1e-6
NEG_BIG = -1e30

FOX_HEADS = 16
FOX_HEAD_DIM = D_MODEL // FOX_HEADS
HG_HEADS = 8
HG_KEY_DIM = 128
HG_VAL_DIM = D_MODEL // HG_HEADS
HG_KEY_TOTAL = HG_HEADS * HG_KEY_DIM
HG_CHUNK = 128
HG_SUB = 16

N_GROUPS = 4
EXPERTS_PER_GROUP = 8
N_EXPERTS = N_GROUPS * EXPERTS_PER_GROUP
TOP_K = 2
D_EXPERT = D_MODEL // 2
EXPERT_ROWS = 512

LANES = 128
SUBLANES = 8
VMEM_LIMIT = 56 * 1024 * 1024


def _dot(a, b):
    return jnp.dot(a, b, preferred_element_type=F32)


def _dot_nt(a, b):
    return lax.dot_general(a, b, (((1,), (1,)), ((), ())), preferred_element_type=F32)


def _dot_tn(a, b):
    return lax.dot_general(a, b, (((0,), (0,)), ((), ())), preferred_element_type=F32)


def _rms(x, gain):
    return x * lax.rsqrt(jnp.mean(x * x, axis=-1, keepdims=True) + EPS) * gain


def _sigmoid(x):
    return 1.0 / (1.0 + jnp.exp(-x))


def _silu(x):
    return x * _sigmoid(x)


def _row_tile(lp):
    for t in (384, 256, 128):
        if lp % t == 0:
            return t
    raise ValueError(lp)


def _params(n_axes):
    return pltpu.CompilerParams(dimension_semantics=("arbitrary",) * n_axes,
                                vmem_limit_bytes=VMEM_LIMIT)


def _full(shape):
    return pl.BlockSpec(shape, lambda *_: (0,) * len(shape))


ATT_BLOCK = 256
LOG2E = 1.4426950408889634

LANE_C_Q = FOX_HEAD_DIM
LANE_C_K = FOX_HEAD_DIM + 3
LANE_KEY_PAD = FOX_HEAD_DIM + 6
LANE_SHIFT = FOX_HEAD_DIM + 7
LANE_ONE_V = FOX_HEAD_DIM


def _bf16_trunc(x):
    bits = lax.bitcast_convert_type(x, jnp.int32) & jnp.int32(-65536)
    return lax.bitcast_convert_type(bits, F32)


def _split3(c):
    hi = _bf16_trunc(c)
    mid = _bf16_trunc(c - hi)
    return hi, mid, c - hi - mid


def _split2(x):
    hi = x.astype(BF16)
    return hi, (x - hi.astype(F32)).astype(BF16)


def _fox_in_kernel(h_ref, gain_ref, wqkv_ref, wf_ref, bf_ref, gq_ref, gk_ref, seg_ref, segt_ref,
                   tri_ref, rowq_ref, rowk_ref, qa_ref, ka_ref, va_ref, carry_ref, *, tm, nt, pad):
    t = pl.program_id(1)

    @pl.when(t == 0)
    def _():
        carry_ref[...] = jnp.zeros_like(carry_ref)

    @pl.when(t >= nt)
    def _():
        qa_ref[...] = jnp.zeros_like(qa_ref)
        ka_ref[...] = jnp.zeros_like(ka_ref)
        va_ref[...] = jnp.zeros_like(va_ref)

    @pl.when(t < nt)
    def _():
        xn = _rms(h_ref[0], gain_ref[...])
        qkv = _dot(xn.astype(BF16), wqkv_ref[...])

        def head_norm(z, g_ref):
            hi, lo = _split2(z * z)
            ms = (_dot(hi, seg_ref[...]) + _dot(lo, seg_ref[...])) * (1.0 / FOX_HEAD_DIM)
            hi, lo = _split2(lax.rsqrt(ms + EPS))
            return z * (_dot(hi, segt_ref[...]) + _dot(lo, segt_ref[...])) * g_ref[...]

        qn = head_norm(qkv[:, :D_MODEL], gq_ref)
        kn = head_norm(qkv[:, D_MODEL:2 * D_MODEL], gk_ref)
        vv = qkv[:, 2 * D_MODEL:]

        x_hi, x_lo = _split2(xn)
        w_hi, w_lo = _split2(wf_ref[...])
        fl = _dot(x_hi, w_hi) + _dot(x_hi, w_lo) + _dot(x_lo, w_hi) + bf_ref[...]
        log_f = jnp.minimum(fl, 0.0) - jnp.log(1.0 + jnp.exp(-jnp.abs(fl)))
        pos16 = t * tm + lax.broadcasted_iota(jnp.int32, log_f.shape, 0)
        log_f = jnp.where(pos16 >= pad, log_f, 0.0)
        tri = tri_ref[...]
        cum = carry_ref[...] + sum(_dot(tri, piece.astype(BF16)) for piece in _split3(log_f))
        carry_ref[...] = cum[tm - 1:tm, :]
        c_hi, c_mid, c_lo = _split3(cum * LOG2E)

        lane = lax.broadcasted_iota(jnp.int32, (tm, LANES), 1)
        pos = t * tm + lax.broadcasted_iota(jnp.int32, (tm, LANES), 0)
        is_head = lane < FOX_HEAD_DIM
        row_k = jnp.where((lane == LANE_KEY_PAD) & (pos < pad), NEG_BIG, rowk_ref[...])
        row_v = jnp.where(lane == LANE_ONE_V, 1.0, 0.0)
        for hd in range(FOX_HEADS):
            col = slice((hd // 2) * LANES, (hd // 2 + 1) * LANES)

            def head(z):
                z = z[:, col]
                return pltpu.roll(z, FOX_HEAD_DIM, axis=1) if hd % 2 else z

            def spread(piece):
                return jnp.broadcast_to(piece[:, hd:hd + 1], (tm, LANES))

            hi, mid, lo = spread(c_hi), spread(c_mid), spread(c_lo)
            aug_q = jnp.where(lane == LANE_C_Q, hi, jnp.where(lane == LANE_C_Q + 1, mid,
                              jnp.where(lane == LANE_C_Q + 2, lo, rowq_ref[...])))
            aug_k = jnp.where(lane == LANE_C_K, -hi, jnp.where(lane == LANE_C_K + 1, -mid,
                              jnp.where(lane == LANE_C_K + 2, -lo, row_k)))
            qa_ref[0, hd] = jnp.where(is_head, head(qn), aug_q).astype(BF16)
            ka_ref[0, hd] = jnp.where(is_head, head(kn), aug_k).astype(BF16)
            va_ref[0, hd] = jnp.where(is_head, head(vv), row_v).astype(BF16)


def _fox_in(h, gain, wqkv, wf, bf, gq, gk, shift, pad):
    bsz, lp, _ = h.shape
    tm = _row_tile(lp)
    nt = lp // tm
    lpa = -(-lp // ATT_BLOCK) * ATT_BLOCK
    head_of = jnp.arange(D_MODEL) // FOX_HEAD_DIM
    seg = (head_of[:, None] == jnp.arange(FOX_HEADS)[None, :]).astype(BF16)
    tri = jnp.tril(jnp.ones((tm, tm), BF16))
    lane = jnp.arange(LANES)
    row_q = jnp.where((lane >= LANE_C_K) & (lane <= LANE_KEY_PAD), 1.0,
                      jnp.where(lane == LANE_SHIFT, -shift * LOG2E, 0.0)).astype(F32)[None, :]
    row_k = (((lane >= LANE_C_Q) & (lane < LANE_C_K)) | (lane == LANE_SHIFT)).astype(F32)[None, :]
    out = pl.BlockSpec((1, FOX_HEADS, tm, LANES), lambda b, t: (b, 0, t, 0))
    return pl.pallas_call(
        functools.partial(_fox_in_kernel, tm=tm, nt=nt, pad=pad),
        grid=(bsz, -(-lpa // tm)),
        in_specs=[pl.BlockSpec((1, tm, D_MODEL), lambda b, t: (b, jnp.minimum(t, nt - 1), 0)),
                  _full((1, D_MODEL)), _full((D_MODEL, 3 * D_MODEL)),
                  _full((D_MODEL, FOX_HEADS)), _full((1, FOX_HEADS)),
                  _full((1, D_MODEL)), _full((1, D_MODEL)),
                  _full((D_MODEL, FOX_HEADS)), _full((FOX_HEADS, D_MODEL)), _full((tm, tm)),
                  _full((1, LANES)), _full((1, LANES))],
        out_specs=[out, out, out],
        out_shape=[jax.ShapeDtypeStruct((bsz, FOX_HEADS, lpa, LANES), BF16)] * 3,
        scratch_shapes=[pltpu.VMEM((1, FOX_HEADS), F32)],
        compiler_params=_params(2),
        name="fox_in",
    )(h, gain, wqkv, wf, bf, gq, gk, seg, seg.T, tri, row_q, row_k)


ATT_Q_STRIPS = 5
ATT_SHIFT_MAX = 40.0


def _flash_kernel(q_ref, k_ref, v_ref, o_ref, acc_ref, *, tq, tk):
    i = pl.program_id(2)
    n_strip = tq // tk
    lane = lax.broadcasted_iota(jnp.int32, (tq, LANES), 1)

    for hd in range(2):
        q = q_ref[0, hd]

        def strip(rows, off, masked):
            s = _dot_nt(q[rows:], k_ref[0, hd, pl.ds(off, tk), :])
            if masked:
                r = lax.broadcasted_iota(jnp.int32, s.shape, 0)
                c = lax.broadcasted_iota(jnp.int32, s.shape, 1)
                s = jnp.where(c <= r, s, NEG_BIG)
            p = jnp.exp2(s).astype(BF16)
            return _dot(p, v_ref[0, hd, pl.ds(off, tk), :])

        def below_diagonal(first, tiles):
            total = None
            for n in range(tiles * n_strip):
                part = strip(0, pl.multiple_of(first * tq + n * tk, tk), False)
                total = part if total is None else total + part
            acc_ref[...] += total

        def tile_quad(j, carry):
            below_diagonal(4 * j, 4)
            return carry

        acc_ref[...] = jnp.zeros_like(acc_ref)
        lax.fori_loop(0, i // 4, tile_quad, 0)

        @pl.when(i % 4 >= 2)
        def _():
            below_diagonal(i // 4 * 4, 2)

        @pl.when(i % 2 == 1)
        def _():
            below_diagonal(i - 1, 1)

        for n in range(n_strip):
            acc_ref[n * tk:, :] += strip(n * tk, pl.multiple_of(i * tq + n * tk, tk), True)
        acc = acc_ref[...]
        denom = acc[:, FOX_HEAD_DIM:FOX_HEAD_DIM + 1]
        out = acc / jnp.where(denom > 0.0, denom, 1.0)
        if hd == 0:
            o_ref[0] = out.astype(o_ref.dtype)
        else:
            shifted = pltpu.roll(out, FOX_HEAD_DIM, axis=1).astype(o_ref.dtype)
            o_ref[0] = jnp.where(lane < FOX_HEAD_DIM, o_ref[0], shifted)


def _flash_online_kernel(q_ref, k_ref, v_ref, o_ref, *, blk):
    i = pl.program_id(2)
    lane = lax.broadcasted_iota(jnp.int32, (blk, LANES), 1)
    outs = []
    for hd in range(2):
        q = q_ref[0, hd]

        def block(j, m, acc, masked):
            off = pl.multiple_of(j * blk, blk)
            s = _dot_nt(q, k_ref[0, hd, pl.ds(off, blk), :])
            if masked:
                r = lax.broadcasted_iota(jnp.int32, s.shape, 0)
                c = lax.broadcasted_iota(jnp.int32, s.shape, 1)
                s = jnp.where(c <= r, s, NEG_BIG)
            m_new = jnp.maximum(m, jnp.max(s, axis=-1, keepdims=True))
            p = jnp.exp2(s - m_new).astype(BF16)
            return m_new, acc * jnp.exp2(m - m_new) + _dot(p, v_ref[0, hd, pl.ds(off, blk), :])

        m0 = jnp.full((blk, 1), NEG_BIG, F32)
        acc0 = jnp.zeros((blk, LANES), F32)
        m, acc = lax.fori_loop(0, i, lambda j, c: block(j, c[0], c[1], False), (m0, acc0))
        _, acc = block(i, m, acc, True)
        outs.append(acc / acc[:, FOX_HEAD_DIM:FOX_HEAD_DIM + 1])
    o_ref[0] = jnp.where(lane < FOX_HEAD_DIM, outs[0],
                         pltpu.roll(outs[1], FOX_HEAD_DIM, axis=1)).astype(o_ref.dtype)


def _flash(qa, ka, va, lp, online):
    bsz, nh, lpa, _ = qa.shape
    tk = ATT_BLOCK
    tq = tk if online or lpa % (ATT_Q_STRIPS * tk) else ATT_Q_STRIPS * tk
    qspec = pl.BlockSpec((1, 2, tq, LANES), lambda b, h, i: (b, h, i, 0))
    kvspec = pl.BlockSpec((1, 2, lpa, LANES), lambda b, h, i: (b, h, 0, 0))
    if online:
        body, scratch, name = functools.partial(_flash_online_kernel, blk=tq), [], "fox_flash_online"
    else:
        body = functools.partial(_flash_kernel, tq=tq, tk=tk)
        scratch, name = [pltpu.VMEM((tq, LANES), F32)], "fox_flash"
    return pl.pallas_call(
        body,
        grid=(bsz, nh // 2, lpa // tq),
        in_specs=[qspec, kvspec, kvspec],
        out_specs=pl.BlockSpec((1, tq, LANES), lambda b, h, i: (b, i, h)),
        out_shape=jax.ShapeDtypeStruct((bsz, lp, D_MODEL), BF16),
        scratch_shapes=scratch,
        compiler_params=_params(3),
        name=name,
    )(qa, ka, va)


def _out_proj_kernel(a_ref, w_ref, h_ref, o_ref, *, tm, pad):
    t = pl.program_id(1)
    y = h_ref[0] + _dot(a_ref[0], w_ref[...])
    pos = t * tm + lax.broadcasted_iota(jnp.int32, y.shape, 0)
    o_ref[0] = jnp.where(pos >= pad, y, 0.0)


def _out_proj(a, w, h, pad):
    bsz, lp, _ = h.shape
    tm = _row_tile(lp)
    row = pl.BlockSpec((1, tm, D_MODEL), lambda b, t: (b, t, 0))
    return pl.pallas_call(
        functools.partial(_out_proj_kernel, tm=tm, pad=pad),
        grid=(bsz, lp // tm),
        in_specs=[row, _full((D_MODEL, D_MODEL)), row],
        out_specs=row,
        out_shape=jax.ShapeDtypeStruct(h.shape, F32),
        compiler_params=_params(2),
        name="out_proj",
    )(a, w, h)


def _hgrn_kernel(*refs, pad, n_sub, fused_moe):
    if fused_moe:
        mix_refs, refs = refs[:6], refs[6:]
    h_ref, gain_ref, win_ref, bf_ref, lb_ref, go_ref, wout_ref, tri_ref, o_ref, state_ref = refs[:10]
    t = pl.program_id(1)

    @pl.when(t == 0)
    def _():
        state_ref[...] = jnp.zeros_like(state_ref)

    x = h_ref[0]
    if fused_moe:
        d0_ref, d1_ref, d0n_ref, d1n_ref, y_hbm, gate_ref = mix_refs
        buf, sem = refs[10:]
        step = pl.program_id(0) * pl.num_programs(1) + t
        steps = pl.num_programs(0) * pl.num_programs(1)
        x = _moe_mix(step, steps, (d0_ref, d1_ref, d0n_ref, d1n_ref), y_hbm, x, gate_ref, buf, sem,
                     n_sub * HG_CHUNK, prefetch_next=False)
        pos_in = t * (n_sub * HG_CHUNK) + lax.broadcasted_iota(jnp.int32, x.shape, 0)
        x = jnp.where(pos_in >= pad, x, 0.0)

        def prefetch_rows(lo):
            for r in range(lo, lo + HG_SUB):
                _mix_issue(d0n_ref, d1n_ref, y_hbm, buf, sem, 1 - step % 2, r)
    else:
        def prefetch_rows(lo):
            del lo
    proj = _dot(_rms(x, gain_ref[...]).astype(BF16), win_ref[...])
    tri = tri_ref[...]
    row = lax.broadcasted_iota(jnp.int32, (HG_CHUNK, HG_KEY_TOTAL), 0)
    causal = (lax.broadcasted_iota(jnp.int32, (HG_CHUNK, HG_CHUNK), 1)
              <= lax.broadcasted_iota(jnp.int32, (HG_CHUNK, HG_CHUNK), 0))
    heads = [slice(hd * HG_KEY_DIM, (hd + 1) * HG_KEY_DIM) for hd in range(HG_HEADS)]

    q_all = _silu(proj[:, :HG_KEY_TOTAL])
    lb = lb_ref[...]
    f = lb + (1.0 - lb) * _sigmoid(proj[:, HG_KEY_TOTAL:2 * HG_KEY_TOTAL] + bf_ref[...])
    kk_all = 1.0 - f
    v_all = proj[:, 2 * HG_KEY_TOTAL:2 * HG_KEY_TOTAL + D_MODEL]
    lf_hi_all, lf_lo_all = _split2(jnp.log(f))
    states = [state_ref[hd] for hd in range(HG_HEADS)]
    chunks = []
    for c in range(n_sub):
        rs = slice(c * HG_CHUNK, (c + 1) * HG_CHUNK)
        q, kk, v = q_all[rs], kk_all[rs], v_all[rs]
        b = _dot(tri, lf_hi_all[rs]) + _dot(tri, lf_lo_all[rs])
        q_dec = q * jnp.exp(b)
        outs = [_dot_nt(q_dec[:, sl], states[hd]) for hd, sl in enumerate(heads)]
        blocks = [[] for _ in heads]
        for sc in range(HG_CHUNK // HG_SUB):
            lo_r, hi_r = sc * HG_SUB, (sc + 1) * HG_SUB
            ref = b[lo_r - 1:lo_r, :] if sc else jnp.zeros((1, HG_KEY_TOTAL), F32)
            qt = q[lo_r:hi_r] * jnp.exp(b[lo_r:hi_r] - ref)
            kt = kk * jnp.exp(jnp.where(row < hi_r, ref - b, NEG_BIG))
            prefetch_rows(c * HG_CHUNK + lo_r)
            for hd, sl in enumerate(heads):
                blocks[hd].append(_dot_nt(qt[:, sl], kt[:, sl]))
        b_end = b[HG_CHUNK - 1:HG_CHUNK, :]
        k_dec = kk * jnp.exp(b_end - b)
        s_dec = jnp.exp(b_end)
        for hd, sl in enumerate(heads):
            scores = jnp.where(causal, jnp.concatenate(blocks[hd], axis=0), 0.0)
            outs[hd] = _rms(outs[hd] + _dot(scores, v[:, sl]), go_ref[...])
            states[hd] = s_dec[:, sl] * states[hd] + _dot_tn(v[:, sl], k_dec[:, sl])
        chunks.append(jnp.concatenate(outs, axis=-1))
    for hd in range(HG_HEADS):
        state_ref[hd] = states[hd]
    gate = _silu(proj[:, 2 * HG_KEY_TOTAL + D_MODEL:])
    o_all = jnp.concatenate(chunks, axis=0)
    y = x + _dot((o_all * gate).astype(BF16), wout_ref[...])
    pos = t * (n_sub * HG_CHUNK) + lax.broadcasted_iota(jnp.int32, y.shape, 0)
    o_ref[0] = jnp.where(pos >= pad, y, 0.0)
    if fused_moe:
        @pl.when(step == steps - 1)
        def _():
            _mix_wait(y_hbm, buf, sem, 1 - step % 2, n_sub * HG_CHUNK)


def _hgrn_layer(h, gain, win, bf, lb, go, wout, pad, moe=None):
    bsz, lp, _ = h.shape
    hg_in = win.shape[1]
    tm = _row_tile(lp)
    nt = lp // tm
    tri = jnp.tril(jnp.ones((HG_CHUNK, HG_CHUNK), BF16))
    row = pl.BlockSpec((1, tm, D_MODEL), lambda b, t: (b, t, 0))
    operands, specs = (), []
    scratch = [pltpu.VMEM((HG_HEADS, HG_VAL_DIM, HG_KEY_DIM), F32)]
    if moe is not None:
        operands, specs, mix_scratch, _ = _mix_operands(*moe, tm, lambda b, t: b * nt + t)
        scratch += mix_scratch
    return pl.pallas_call(
        functools.partial(_hgrn_kernel, pad=pad, n_sub=tm // HG_CHUNK, fused_moe=moe is not None),
        grid=(bsz, nt),
        in_specs=specs + [row, _full((1, D_MODEL)), _full((D_MODEL, hg_in)), _full((1, HG_KEY_TOTAL)),
                          _full((1, HG_KEY_TOTAL)), _full((1, HG_VAL_DIM)), _full((D_MODEL, D_MODEL)),
                          _full((HG_CHUNK, HG_CHUNK))],
        out_specs=row,
        out_shape=jax.ShapeDtypeStruct(h.shape, F32),
        scratch_shapes=scratch,
        compiler_params=_params(2),
        name="hgrn_layer",
    )(*operands, h, gain, win, bf, lb, go, wout, tri)


def _router_kernel(h_ref, gain_ref, w_ref, b_ref, tri_ref, idx_ref, gate_ref, xn_ref, count_ref):
    @pl.when(pl.program_id(0) == 0)
    def _():
        count_ref[...] = jnp.zeros_like(count_ref)

    xn = _rms(h_ref[...], gain_ref[...])
    xn_ref[...] = xn
    x_hi, x_lo = _split2(xn)
    w_hi, w_lo = _split2(w_ref[...])
    logit = _dot(x_hi, w_hi) + _dot(x_hi, w_lo) + _dot(x_lo, w_hi) + b_ref[...]
    lane = lax.broadcasted_iota(jnp.int32, logit.shape, 1)

    def first_lane_of(hit):
        return jnp.min(jnp.where(hit, lane, LANES), axis=-1, keepdims=True)

    is_grp = (lane >= N_EXPERTS) & (lane < N_EXPERTS + N_GROUPS)
    g_logit = jnp.where(is_grp, logit, -jnp.inf)
    g_max = jnp.max(g_logit, axis=-1, keepdims=True)
    g_idx = first_lane_of(g_logit == g_max) - N_EXPERTS
    g_gate = 1.0 / jnp.sum(jnp.exp(g_logit - g_max), axis=-1, keepdims=True)
    in_grp = (lane >= g_idx * EXPERTS_PER_GROUP) & (lane < (g_idx + 1) * EXPERTS_PER_GROUP)
    e1 = jnp.where(in_grp, logit, -jnp.inf)
    v1 = jnp.max(e1, axis=-1, keepdims=True)
    i1 = first_lane_of(e1 == v1)
    e2 = jnp.where(lane == i1, -jnp.inf, e1)
    v2 = jnp.max(e2, axis=-1, keepdims=True)
    i2 = first_lane_of(e2 == v2)
    z = jnp.exp(v2 - v1)
    w1 = g_gate / (1.0 + z)
    gate_ref[...] = jnp.where(lane == 0, w1, jnp.where(lane == 1, w1 * z, 0.0))
    pick1 = lane == i1
    pick2 = lane == i2
    onehot = jnp.where(pick1 | pick2, 1.0, 0.0)
    before = count_ref[...] + _dot(tri_ref[...], onehot.astype(BF16))
    r1 = jnp.sum(jnp.where(pick1, before, 0.0), axis=-1, keepdims=True).astype(jnp.int32)
    r2 = jnp.sum(jnp.where(pick2, before, 0.0), axis=-1, keepdims=True).astype(jnp.int32)
    tm = onehot.shape[0]
    count_ref[...] = before[tm - 1:tm, :] + onehot[tm - 1:tm, :]
    table = jnp.where(lane == 0, i1, jnp.where(lane == 1, i2,
                      jnp.where(lane == 2, r1, jnp.where(lane == 3, r2, 0)))).astype(F32)
    idx_ref[...] = table.T[:SUBLANES, :].astype(jnp.int32)


def _router(h2, gain, w_cat, b_cat):
    n = h2.shape[0]
    tm = _row_tile(n)
    row = pl.BlockSpec((tm, D_MODEL), lambda t: (t, 0))
    lanes = pl.BlockSpec((tm, LANES), lambda t: (t, 0))
    tri = jnp.tril(jnp.ones((tm, tm), BF16), -1)
    return pl.pallas_call(
        _router_kernel,
        grid=(n // tm,),
        in_specs=[row, _full((1, D_MODEL)), _full((D_MODEL, LANES)), _full((1, LANES)), _full((tm, tm))],
        out_specs=[pl.BlockSpec((SUBLANES, tm), lambda t: (0, t)), lanes, row,
                   _full((1, LANES))],
        out_shape=[jax.ShapeDtypeStruct((SUBLANES, n), jnp.int32),
                   jax.ShapeDtypeStruct((n, LANES), F32),
                   jax.ShapeDtypeStruct((n, D_MODEL), F32),
                   jax.ShapeDtypeStruct((1, LANES), F32)],
        compiler_params=_params(1),
        name="moe_router",
    )(h2, gain, w_cat, b_cat, tri)


def _row_copy(src_ref, dst_ref, sem, src_row, dst_row):
    return pltpu.make_async_copy(src_ref.at[pl.ds(src_row, 1)], dst_ref.at[pl.ds(dst_row, 1)], sem)


def _dispatch_kernel(fill_ref, nb_ref, d0_ref, d1_ref, x_ref, o_hbm, zero_ref, sem, *, tm, n_blocks):
    @pl.when(pl.program_id(0) == 0)
    def _():
        zero_ref[...] = jnp.zeros_like(zero_ref)

        def fill_copy(row):
            return pltpu.make_async_copy(zero_ref, o_hbm.at[pl.ds(row, EXPERT_ROWS)], sem.at[2])

        def tail_start(b, carry):
            fill_copy(pl.multiple_of(b * EXPERT_ROWS, EXPERT_ROWS)).start()
            return carry

        def tail_wait(b, carry):
            fill_copy(pl.multiple_of(b * EXPERT_ROWS, EXPERT_ROWS)).wait()
            return carry

        for e in range(N_EXPERTS):
            fill_copy(pl.multiple_of(fill_ref[e], SUBLANES)).start()
        for e in range(N_EXPERTS):
            fill_copy(pl.multiple_of(fill_ref[e], SUBLANES)).wait()
        lax.fori_loop(nb_ref[0], n_blocks + 1, tail_start, 0)
        lax.fori_loop(nb_ref[0], n_blocks + 1, tail_wait, 0)

    def issue(r, carry):
        _row_copy(x_ref, o_hbm, sem.at[0], r, d0_ref[0, 0, r]).start()
        _row_copy(x_ref, o_hbm, sem.at[1], r, d1_ref[0, 0, r]).start()
        return carry

    lax.fori_loop(0, tm, issue, 0)
    pltpu.make_async_copy(x_ref, o_hbm.at[pl.ds(0, tm)], sem.at[0]).wait()
    pltpu.make_async_copy(x_ref, o_hbm.at[pl.ds(0, tm)], sem.at[1]).wait()


def _dispatch(xn3, dest, fill_start, n_used, n_blocks):
    n = xn3.shape[0]
    tm = 2 * _row_tile(n) if n % (2 * _row_tile(n)) == 0 else _row_tile(n)
    steps = n // tm
    idx = pl.BlockSpec((1, 1, tm), lambda t, fill, nb: (t, 0, 0), memory_space=pltpu.SMEM)
    return pl.pallas_call(
        functools.partial(_dispatch_kernel, tm=tm, n_blocks=n_blocks),
        grid_spec=pltpu.PrefetchScalarGridSpec(
            num_scalar_prefetch=2,
            grid=(steps,),
            in_specs=[idx, idx, pl.BlockSpec((tm, D_MODEL), lambda t, fill, nb: (t, 0))],
            out_specs=pl.BlockSpec(memory_space=pl.ANY),
            scratch_shapes=[pltpu.VMEM((EXPERT_ROWS, D_MODEL), F32),
                            pltpu.SemaphoreType.DMA((3,))]),
        out_shape=jax.ShapeDtypeStruct(((n_blocks + 1) * EXPERT_ROWS, D_MODEL), F32),
        compiler_params=_params(1),
        name="moe_dispatch",
    )(fill_start, n_used, dest[0].reshape(steps, 1, tm), dest[1].reshape(steps, 1, tm), xn3)


def _expert_kernel(be_ref, nb_ref, x_ref, wup_ref, wdn_ref, y_ref):
    del be_ref

    used = pl.program_id(0) < nb_ref[0]

    @pl.when(jnp.logical_not(used))
    def _():
        y_ref[...] = jnp.zeros_like(y_ref)

    @pl.when(used)
    def _():
        gu = _dot(x_ref[...], wup_ref[0])
        act = _silu(gu[:, :D_EXPERT]) * gu[:, D_EXPERT:]
        y_ref[...] = _dot(act, wdn_ref[0])


def _experts(x_pad3, block_expert, n_used, w_up, w_down, n_blocks):
    blk = pl.BlockSpec((EXPERT_ROWS, D_MODEL), lambda i, be, nb: (i, 0))
    return pl.pallas_call(
        _expert_kernel,
        grid_spec=pltpu.PrefetchScalarGridSpec(
            num_scalar_prefetch=2,
            grid=(n_blocks,),
            in_specs=[blk,
                      pl.BlockSpec((1, D_MODEL, 2 * D_EXPERT), lambda i, be, nb: (be[i], 0, 0)),
                      pl.BlockSpec((1, D_EXPERT, D_MODEL), lambda i, be, nb: (be[i], 0, 0))],
            out_specs=blk),
        out_shape=jax.ShapeDtypeStruct((n_blocks * EXPERT_ROWS, D_MODEL), F32),
        compiler_params=_params(1),
        name="moe_experts",
    )(block_expert, n_used, x_pad3, w_up, w_down)


def _mix_issue(i0_ref, i1_ref, y_hbm, buf, sem, into, r):
    _row_copy(y_hbm, buf.at[2 * into], sem.at[2 * into], i0_ref[0, 0, r], r).start()
    _row_copy(y_hbm, buf.at[2 * into + 1], sem.at[2 * into + 1], i1_ref[0, 0, r], r).start()


def _mix_wait(y_hbm, buf, sem, slot, tm):
    for k in range(TOP_K):
        pltpu.make_async_copy(y_hbm.at[pl.ds(0, tm)], buf.at[2 * slot + k], sem.at[2 * slot + k]).wait()


def _moe_mix(step, steps, dest_refs, y_hbm, h, gate_ref, buf, sem, tm, prefetch_next=True):
    d0_ref, d1_ref, d0n_ref, d1n_ref = dest_refs
    slot = step % 2

    def fetch(i0_ref, i1_ref, into):
        def issue(r, carry):
            _mix_issue(i0_ref, i1_ref, y_hbm, buf, sem, into, r)
            return carry

        lax.fori_loop(0, tm, issue, 0)

    @pl.when(step == 0)
    def _():
        fetch(d0_ref, d1_ref, slot)

    if prefetch_next:
        @pl.when(step + 1 < steps)
        def _():
            fetch(d0n_ref, d1n_ref, 1 - slot)

    _mix_wait(y_hbm, buf, sem, slot, tm)
    return h + gate_ref[:, 0:1] * buf[2 * slot] + gate_ref[:, 1:2] * buf[2 * slot + 1]


def _combine_kernel(d0_ref, d1_ref, d0n_ref, d1n_ref, y_hbm, h_ref, gate_ref, o_ref, buf, sem,
                    *, tm, steps, lp, pad):
    t = pl.program_id(0)
    y = _moe_mix(t, steps, (d0_ref, d1_ref, d0n_ref, d1n_ref), y_hbm, h_ref[...], gate_ref, buf, sem, tm)
    pos = (t * tm + lax.broadcasted_iota(jnp.int32, (tm, D_MODEL), 0)) % lp
    o_ref[...] = jnp.where(pos >= pad, y, 0.0)


def _mix_operands(y, dest, gate, tm, step_of):
    steps = dest.shape[1] // tm

    def idx_spec(shift):
        return pl.BlockSpec((1, 1, tm), lambda *g: (jnp.minimum(step_of(*g) + shift, steps - 1), 0, 0),
                            memory_space=pltpu.SMEM)

    d0 = dest[0].reshape(steps, 1, tm)
    d1 = dest[1].reshape(steps, 1, tm)
    specs = [idx_spec(0), idx_spec(0), idx_spec(1), idx_spec(1), pl.BlockSpec(memory_space=pl.ANY),
             pl.BlockSpec((tm, LANES), lambda *g: (step_of(*g), 0))]
    scratch = [pltpu.VMEM((2 * TOP_K, tm, D_MODEL), F32), pltpu.SemaphoreType.DMA((2 * TOP_K,))]
    return (d0, d1, d0, d1, y, gate), specs, scratch, steps


def _combine(h, moe, pad):
    bsz, lp, _ = h.shape
    n = bsz * lp
    tm = _row_tile(lp)
    (d0, d1, d0n, d1n, y, gate), specs, scratch, steps = _mix_operands(*moe, tm, lambda t: t)
    row = pl.BlockSpec((tm, D_MODEL), lambda t: (t, 0))
    return pl.pallas_call(
        functools.partial(_combine_kernel, tm=tm, steps=steps, lp=lp, pad=pad),
        grid=(steps,),
        in_specs=specs[:5] + [row, specs[5]],
        out_specs=row,
        out_shape=jax.ShapeDtypeStruct((n, D_MODEL), F32),
        scratch_shapes=scratch,
        compiler_params=_params(1),
        name="moe_combine",
    )(d0, d1, d0n, d1n, y, h.reshape(n, D_MODEL), gate).reshape(h.shape)


def _moe_layer(h, gain, w_grp, b_grp, w_rt, b_rt, w_up, w_down, layer):
    bsz, lp, _ = h.shape
    n = bsz * lp
    h2 = h.reshape(n, D_MODEL)
    fill = LANES - N_EXPERTS - N_GROUPS
    w_cat = jnp.concatenate([w_rt, w_grp, jnp.zeros((D_MODEL, fill), F32)], axis=1)
    b_cat = jnp.concatenate([b_rt, b_grp, jnp.zeros((fill,), F32)])[None, :]
    idx, gate, xn3, counts = _router(h2, gain, w_cat, b_cat)

    counts = counts[0, :N_EXPERTS].astype(jnp.int32)
    padded = (counts + EXPERT_ROWS - 1) // EXPERT_ROWS * EXPERT_ROWS
    pend = jnp.cumsum(padded)
    pstart = pend - padded
    seg_start = jnp.sum(jnp.where(idx[:TOP_K, :, None] == jnp.arange(N_EXPERTS), pstart, 0), axis=-1)
    dest = seg_start + idx[TOP_K:2 * TOP_K]
    n_blocks = (n * TOP_K + N_EXPERTS * (EXPERT_ROWS - 1) + EXPERT_ROWS - 1) // EXPERT_ROWS
    block_start = jnp.arange(n_blocks, dtype=jnp.int32) * EXPERT_ROWS
    block_expert = jnp.minimum(jnp.sum(block_start[:, None] >= pend[None, :], axis=1),
                               N_EXPERTS - 1).astype(jnp.int32) + layer * N_EXPERTS
    n_used = (pend[N_EXPERTS - 1:] // EXPERT_ROWS).astype(jnp.int32)

    fill_start = (pstart + counts) // SUBLANES * SUBLANES
    x_pad3 = _dispatch(xn3, dest, fill_start, n_used, n_blocks)
    return _experts(x_pad3, block_expert, n_used, w_up, w_down, n_blocks), dest, gate


def _lower_bounds(lb_logits):
    p = jax.nn.softmax(lb_logits.astype(F32), axis=0)
    cum = jnp.cumsum(p, axis=0)
    return cum - cum[:1]


@jax.jit
def kernel(x, meta_tokens, fox_norm, fox_w_in, fox_b_f, fox_q_norm, fox_k_norm, fox_w_out, hg_norm, hg_w_in, hg_b_f, hg_lb_logits, hg_o_norm, hg_w_out, moe_norm, moe_w_grp, moe_b_grp, moe_w_rt, moe_b_rt, moe_w_up, moe_w_down):
    bsz, seq, _ = x.shape
    depth = moe_norm.shape[0]
    length = N_META + seq
    pad = (-length) % ROW_ALIGN
    meta = jnp.broadcast_to(meta_tokens[None].astype(x.dtype), (bsz, N_META, D_MODEL))
    h = jnp.concatenate([jnp.zeros((bsz, pad, D_MODEL), x.dtype), meta, x], axis=1)
    lower_bounds = _lower_bounds(hg_lb_logits)
    scale = FOX_HEAD_DIM ** -0.5
    moe = None
    for i in range(depth):
        j = i // 2
        if i % 2 == 0:
            if moe is not None:
                h = _combine(h, moe, pad)
            shift = FOX_HEAD_DIM * scale * jnp.max(jnp.abs(fox_q_norm[j])) * jnp.max(jnp.abs(fox_k_norm[j]))
            qa, ka, va = _fox_in(
                h, fox_norm[j][None, :], fox_w_in[j][:, :3 * D_MODEL].astype(BF16),
                fox_w_in[j][:, 3 * D_MODEL:], fox_b_f[j][None, :],
                jnp.tile(fox_q_norm[j] * (scale * LOG2E), FOX_HEADS)[None, :],
                jnp.tile(fox_k_norm[j], FOX_HEADS)[None, :], shift, pad)
            o = lax.cond(shift <= ATT_SHIFT_MAX,
                         functools.partial(_flash, lp=h.shape[1], online=False),
                         functools.partial(_flash, lp=h.shape[1], online=True), qa, ka, va)
            h = _out_proj(o, fox_w_out[j].astype(BF16), h, pad)
        else:
            h = _hgrn_layer(h, hg_norm[j][None, :], hg_w_in[j].astype(BF16), hg_b_f[j][None, :],
                            lower_bounds[i][None, :], hg_o_norm[j][None, :],
                            hg_w_out[j].astype(BF16), pad, moe)
        moe = _moe_layer(h, moe_norm[i][None, :], moe_w_grp[i], moe_b_grp[i], moe_w_rt[i], moe_b_rt[i],
                         moe_w_up.reshape(depth * N_EXPERTS, D_MODEL, 2 * D_EXPERT),
                         moe_w_down.reshape(depth * N_EXPERTS, D_EXPERT, D_MODEL), i)
    return _combine(h, moe, pad)[:, pad + N_META:]
```
